```python
import math
import jax, jax.numpy as jnp
from jax import lax
import numpy as np

D_MODEL = 2048
BATCH = 1
SEQ = 16384
DEPTH = 1

CHUNK = 64
Q_BLOCK = 128
EPS = 1e-6

ATT_HEADS = 8
ATT_KV_HEADS = 2
HEAD_DIM = 128
ATT_WIDTH = ATT_HEADS * HEAD_DIM
KV_WIDTH = ATT_KV_HEADS * HEAD_DIM
IDX_HEADS = 16
IDX_DIM = 64
INDEX_TOPK = 256
ROPE_THETA = 500000.0
ROPE_FRACTION = 4

SSM_GROUP_CH = 16
SSM_GROUPS = 32
SSM_WIDTH = SSM_GROUPS * SSM_GROUP_CH
SSM_STATE = 64
DT_MIN = 0.001
DT_MAX = 0.1

N_BRANCH = 2
SPLITS = (ATT_WIDTH, KV_WIDTH, KV_WIDTH, ATT_WIDTH,
          IDX_HEADS * IDX_DIM, IDX_DIM, IDX_HEADS,
          SSM_WIDTH, SSM_WIDTH,
          N_BRANCH * D_MODEL)
IN_WIDTH = sum(SPLITS)

kernel_name = "hybrid_dsa_s5_gated_merge"


def rms_norm(x, g):
    xf = x.astype(jnp.float32)
    y = xf * lax.rsqrt(jnp.mean(xf * xf, axis=-1, keepdims=True) + EPS)
    return (y * g.astype(jnp.float32)).astype(x.dtype)


def partial_rotary(x, positions):
    rd = x.shape[-1] // ROPE_FRACTION
    half = rd // 2
    inv_freq = jnp.exp(-math.log(ROPE_THETA) * jnp.arange(half, dtype=jnp.float32) * (2.0 / rd))
    ang = positions.astype(jnp.float32)[..., None] * inv_freq
    cos = jnp.cos(ang)[:, :, None, :]
    sin = jnp.sin(ang)[:, :, None, :]
    xf = x.astype(jnp.float32)
    x1, x2 = xf[..., :half], xf[..., half:rd]
    rot = jnp.concatenate([x1 * cos - x2 * sin, x2 * cos + x1 * sin], axis=-1).astype(x.dtype)
    return jnp.concatenate([rot, x[..., rd:]], axis=-1)


def dsa_attention(q, k, v, q_idx, k_idx, w_idx):
    b, l = q.shape[:2]
    n_blk = l // Q_BLOCK
    topk = min(INDEX_TOPK, l // 4)
    grp = ATT_HEADS // ATT_KV_HEADS
    key_chunk = jnp.arange(l) // CHUNK

    def to_blocks(a):
        a = a.reshape((b, n_blk, Q_BLOCK) + a.shape[2:])
        return jnp.moveaxis(a, 1, 0)

    q_blocks = to_blocks(q.reshape(b, l, ATT_KV_HEADS, grp, HEAD_DIM))
    qi_blocks = to_blocks(q_idx)
    w_blocks = to_blocks(w_idx)
    gather = jax.vmap(lambda src, ids: src[ids])

    def one_block(args):
        q_blk, qi_blk, w_blk, blk = args
        q_chunk = (blk * Q_BLOCK + jnp.arange(Q_BLOCK)) // CHUNK
        admissible = key_chunk[None, :] <= q_chunk[:, None]
        logits = jnp.einsum("bqhd,bsd->bqhs", qi_blk, k_idx) * (IDX_DIM ** -0.5)
        score = jnp.einsum("bqhs,bqh->bqs", jax.nn.relu(logits), w_blk).astype(jnp.float32)
        score = jnp.where(admissible[None], score, -jnp.inf)
        _, sel = lax.top_k(score, topk)
        valid = (sel // CHUNK) <= q_chunk[None, :, None]
        k_sel = gather(k, sel)
        v_sel = gather(v, sel)
        s = jnp.einsum("bqhgd,bqkhd->bqhgk", q_blk, k_sel).astype(jnp.float32) * (HEAD_DIM ** -0.5)
        s = jnp.where(valid[:, :, None, None, :], s, -jnp.inf)
        p = jax.nn.softmax(s, axis=-1).astype(v.dtype)
        return jnp.einsum("bqhgk,bqkhd->bqhgd", p, v_sel)

    out = lax.map(one_block, (q_blocks, qi_blocks, w_blocks, jnp.arange(n_blk)))
    return jnp.moveaxis(out, 0, 1).reshape(b, l, ATT_WIDTH)


def s5_combine(e1, e2):
    a1r, a1i, b1r, b1i = e1
    a2r, a2i, b2r, b2i = e2
    return (a2r * a1r - a2i * a1i,
            a2r * a1i + a2i * a1r,
            a2r * b1r - a2i * b1i + b2r,
            a2r * b1i + a2i * b1r + b2i)


def s5_ssm(u, a_re, a_im, log_dt, b_re, b_im, c_re, c_im, d_skip):
    bsz, l = u.shape[:2]
    ug = u.reshape(bsz, l, SSM_GROUPS, SSM_GROUP_CH).astype(jnp.float32)
    dt = jnp.exp(log_dt.astype(jnp.float32))[:, None]
    ar = a_re.astype(jnp.float32)
    ai = a_im.astype(jnp.float32)
    mag = jnp.exp(ar * dt)
    lb_re = mag * jnp.cos(ai * dt)
    lb_im = mag * jnp.sin(ai * dt)
    den = ar * ar + ai * ai
    coef_re = ((lb_re - 1.0) * ar + lb_im * ai) / den
    coef_im = (lb_im * ar - (lb_re - 1.0) * ai) / den
    br = b_re.astype(jnp.float32)
    bi = b_im.astype(jnp.float32)
    bb_re = coef_re[..., None] * br - coef_im[..., None] * bi
    bb_im = coef_re[..., None] * bi + coef_im[..., None] * br
    bu_re = jnp.einsum("blgc,gpc->blgp", ug, bb_re)
    bu_im = jnp.einsum("blgc,gpc->blgp", ug, bb_im)
    a_full_re = jnp.broadcast_to(lb_re, bu_re.shape)
    a_full_im = jnp.broadcast_to(lb_im, bu_im.shape)
    _, _, x_re, x_im = lax.associative_scan(s5_combine, (a_full_re, a_full_im, bu_re, bu_im), axis=1)
    y = (jnp.einsum("blgp,gcp->blgc", x_re, c_re.astype(jnp.float32))
         - jnp.einsum("blgp,gcp->blgc", x_im, c_im.astype(jnp.float32))
         + d_skip.astype(jnp.float32) * ug)
    return y.reshape(bsz, l, SSM_WIDTH).astype(u.dtype)


def hybrid_layer(x, positions, g_norm, w_in, b_merge, a_re, a_im, log_dt, ssm_b_re, ssm_b_im,
                 ssm_c_re, ssm_c_im, ssm_d, w_glu, b_glu, w_att_out, w_ssm_out, w_out):
    bsz, l, _ = x.shape
    h = rms_norm(x, g_norm)
    proj = h @ w_in
    offsets = np.cumsum(SPLITS)[:-1].tolist()
    q, k, v, att_gate, q_idx, k_idx, w_idx, u, ssm_gate, merge = jnp.split(proj, offsets, axis=-1)

    q = partial_rotary(q.reshape(bsz, l, ATT_HEADS, HEAD_DIM), positions)
    k = partial_rotary(k.reshape(bsz, l, ATT_KV_HEADS, HEAD_DIM), positions)
    v = v.reshape(bsz, l, ATT_KV_HEADS, HEAD_DIM)
    q_idx = partial_rotary(q_idx.reshape(bsz, l, IDX_HEADS, IDX_DIM), positions)
    k_idx = partial_rotary(k_idx.reshape(bsz, l, 1, IDX_DIM), positions)[:, :, 0]
    w_idx = w_idx * (IDX_HEADS ** -0.5)
    att = dsa_attention(q, k, v, q_idx, k_idx, w_idx)
    y_a = (att * jax.nn.silu(att_gate)) @ w_att_out

    s = jax.nn.gelu(s5_ssm(u, a_re, a_im, log_dt, ssm_b_re, ssm_b_im, ssm_c_re, ssm_c_im, ssm_d))
    s = s * jax.nn.sigmoid(s @ w_glu + b_glu)
    y_b = (s * jax.nn.silu(ssm_gate)) @ w_ssm_out

    gates = jax.nn.sigmoid(merge + b_merge).reshape(bsz, l, N_BRANCH, D_MODEL)
    mixed = gates[:, :, 0] * y_a + gates[:, :, 1] * y_b
    return x + mixed @ w_out


def setup_inputs(seed: int = 0) -> dict:
    key = jax.random.key(seed)
    ks = jax.random.split(key, 20)
    f32 = jnp.float32

    def nrm(k, shape, scale):
        return jax.random.normal(k, shape, f32) * scale

    x = jax.random.normal(ks[0], (BATCH, SEQ, D_MODEL), f32)
    offset = jax.random.randint(ks[1], (BATCH, 1), 0, 4096, dtype=jnp.int32)
    positions = offset + jnp.arange(SEQ, dtype=jnp.int32)[None, :]
    g_norm = 1.0 + nrm(ks[2], (DEPTH, D_MODEL), 0.01)
    w_in = nrm(ks[3], (DEPTH, D_MODEL, IN_WIDTH), D_MODEL ** -0.5)
    b_merge = nrm(ks[4], (DEPTH, N_BRANCH * D_MODEL), 0.01)
    a_re = -0.5 + nrm(ks[5], (DEPTH, SSM_GROUPS, SSM_STATE), 0.01)
    a_im = (math.pi * jnp.arange(SSM_STATE, dtype=f32))[None, None, :] + nrm(ks[6], (DEPTH, SSM_GROUPS, SSM_STATE), 0.01)
    log_dt = jax.random.uniform(ks[7], (DEPTH, SSM_GROUPS), f32, math.log(DT_MIN), math.log(DT_MAX))
    ssm_b_re = nrm(ks[8], (DEPTH, SSM_GROUPS, SSM_STATE, SSM_GROUP_CH), (2 * SSM_GROUP_CH) ** -0.5)
    ssm_b_im = nrm(ks[9], (DEPTH, SSM_GROUPS, SSM_STATE, SSM_GROUP_CH), (2 * SSM_GROUP_CH) ** -0.5)
    ssm_c_re = nrm(ks[10], (DEPTH, SSM_GROUPS, SSM_GROUP_CH, SSM_STATE), SSM_STATE ** -0.5)
    ssm_c_im = nrm(ks[11], (DEPTH, SSM_GROUPS, SSM_GROUP_CH, SSM_STATE), SSM_STATE ** -0.5)
    ssm_d = nrm(ks[12], (DEPTH, SSM_GROUPS, SSM_GROUP_CH), 1.0)
    w_glu = nrm(ks[13], (DEPTH, SSM_WIDTH, SSM_WIDTH), SSM_WIDTH ** -0.5)
    b_glu = nrm(ks[14], (DEPTH, SSM_WIDTH), 0.01)
    w_att_out = nrm(ks[15], (DEPTH, ATT_WIDTH, D_MODEL), ATT_WIDTH ** -0.5)
    w_ssm_out = nrm(ks[16], (DEPTH, SSM_WIDTH, D_MODEL), SSM_WIDTH ** -0.5)
    w_out = nrm(ks[17], (DEPTH, D_MODEL, D_MODEL), D_MODEL ** -0.5)
    g_final = 1.0 + nrm(ks[18], (D_MODEL,), 0.01)
    return {"x": x, "positions": positions, "g_norm": g_norm, "w_in": w_in, "b_merge": b_merge,
            "a_re": a_re, "a_im": a_im, "log_dt": log_dt, "ssm_b_re": ssm_b_re, "ssm_b_im": ssm_b_im,
            "ssm_c_re": ssm_c_re, "ssm_c_im": ssm_c_im, "ssm_d": ssm_d, "w_glu": w_glu, "b_glu": b_glu,
            "w_att_out": w_att_out, "w_ssm_out": w_ssm_out, "w_out": w_out, "g_final": g_final}


def reference(x, positions, g_norm, w_in, b_merge, a_re, a_im, log_dt, ssm_b_re, ssm_b_im,
              ssm_c_re, ssm_c_im, ssm_d, w_glu, b_glu, w_att_out, w_ssm_out, w_out, g_final):
    for i in range(DEPTH):
        x = hybrid_layer(x, positions, g_norm[i], w_in[i], b_merge[i], a_re[i], a_im[i], log_dt[i],
                         ssm_b_re[i], ssm_b_im[i], ssm_c_re[i], ssm_c_im[i], ssm_d[i], w_glu[i], b_glu[i],
                         w_att_out[i], w_ssm_out[i], w_out[i])
    return rms_norm(x, g_final)
```

```python
import functools
import math

import jax
import jax.numpy as jnp
from jax import lax
from jax.experimental import pallas as pl
from jax.experimental.pallas import tpu as pltpu

D_MODEL = 2048
CHUNK = 64
EPS = 1e-6
ATT_HEADS = 8
ATT_KV_HEADS = 2
HEAD_DIM = 128
ATT_WIDTH = ATT_HEADS * HEAD_DIM
KV_WIDTH = ATT_KV_HEADS * HEAD_DIM
IDX_HEADS = 16
IDX_DIM = 64
INDEX_TOPK = 256
ROPE_THETA = 500000.0
ROPE_FRACTION = 4
SSM_GROUP_CH = 16
SSM_GROUPS = 32
SSM_WIDTH = SSM_GROUPS * SSM_GROUP_CH
SSM_STATE = 64
N_BRANCH = 2
SPLITS = (ATT_WIDTH, KV_WIDTH, KV_WIDTH, ATT_WIDTH, IDX_HEADS * IDX_DIM, IDX_DIM, IDX_HEADS,
          SSM_WIDTH, SSM_WIDTH, N_BRANCH * D_MODEL)

LANES = 128
ROPE_FREQS = HEAD_DIM // ROPE_FRACTION // 2
VMEM_LIMIT = 56 * 1024 * 1024

Q_TILE = 256
KEY_SUB = 64
KEY_TILE = 256
SSM_T = 64
LOG2E = 1.4426950408889634
NEG_BIG = -1e30
INT_MIN = -2 ** 31

F32 = jnp.float32
BF16 = jnp.bfloat16


def _params(semantics):
    return pltpu.CompilerParams(dimension_semantics=semantics, vmem_limit_bytes=VMEM_LIMIT)


def _resident(shape):
    zeros = (0,) * len(shape)
    return pl.BlockSpec(shape, lambda *_: zeros, pipeline_mode=pl.Buffered(1))


def _rope_kernel(pos_ref, freq_ref, cos_ref, sin_ref):
    ang = freq_ref[...] * pos_ref[...].astype(F32)
    cos_ref[...] = jnp.cos(ang)
    sin_ref[...] = jnp.sin(ang)


def _rope_tables(positions):
    l = positions.shape[0]
    rd = HEAD_DIM // ROPE_FRACTION
    inv_freq = jnp.exp(-math.log(ROPE_THETA) * jnp.arange(ROPE_FREQS, dtype=F32) * (2.0 / rd))
    cos_t, sin_t = pl.pallas_call(
        _rope_kernel,
        out_shape=(jax.ShapeDtypeStruct((ROPE_FREQS, l), F32),) * 2,
        name="rope_tables",
    )(positions.reshape(1, l), inv_freq.reshape(ROPE_FREQS, 1))
    cos_a, sin_a = cos_t.T, sin_t.T
    one = lambda n: jnp.ones((l, n), F32)
    zero = lambda n: jnp.zeros((l, n), F32)
    c_a = jnp.concatenate([cos_a, cos_a, one(HEAD_DIM - 2 * ROPE_FREQS)], axis=1)
    s1_a = jnp.concatenate([-sin_a, zero(HEAD_DIM - ROPE_FREQS)], axis=1)
    s2_a = jnp.concatenate([zero(ROPE_FREQS), sin_a, zero(HEAD_DIM - 2 * ROPE_FREQS)], axis=1)
    half_b = ROPE_FREQS // 2
    cos_b, sin_b = cos_a[:, 0::2], sin_a[:, 0::2]
    c_b = jnp.concatenate([cos_b, cos_b, one(IDX_DIM - 2 * half_b)], axis=1)
    s1_b = jnp.concatenate([-sin_b, zero(IDX_DIM - half_b)], axis=1)
    s2_b = jnp.concatenate([zero(half_b), sin_b, zero(IDX_DIM - 2 * half_b)], axis=1)
    tile2 = lambda t: jnp.concatenate([t, t], axis=1)
    return (c_a, s1_a, s2_a), (tile2(c_b), tile2(s1_b), tile2(s2_b))


def _rmsnorm_kernel(x_ref, g_ref, o_ref):
    x = x_ref[...]
    y = x * lax.rsqrt(jnp.mean(x * x, axis=-1, keepdims=True) + EPS)
    o_ref[...] = (y * g_ref[...]).astype(o_ref.dtype)


def _rmsnorm(x, g, out_dtype, tm=512):
    l, d = x.shape
    return pl.pallas_call(
        _rmsnorm_kernel,
        grid=(l // tm,),
        in_specs=[pl.BlockSpec((tm, d), lambda i: (i, 0)), pl.BlockSpec((1, d), lambda i: (0, 0))],
        out_specs=pl.BlockSpec((tm, d), lambda i: (i, 0)),
        out_shape=jax.ShapeDtypeStruct((l, d), out_dtype),
        compiler_params=_params(("parallel",)),
        name="rmsnorm",
    )(x, g.reshape(1, d))


def _silu(x):
    return x * jax.nn.sigmoid(x)


def _rotate(xb, c, s1, s2, shift):
    return xb * c + pltpu.roll(xb, LANES - shift, 1) * s1 + pltpu.roll(xb, shift, 1) * s2


def _proj_kernel(*refs, segments, n_tab, has_bias):
    h_ref, w_ref = refs[0], refs[1]
    pos = 2
    bias_ref = None
    if has_bias:
        bias_ref = refs[pos]
        pos += 1
    tabs = [r[...] for r in refs[pos:pos + n_tab]]
    o_ref = refs[pos + n_tab]
    acc = jnp.dot(h_ref[...], w_ref[...], preferred_element_type=F32)
    if has_bias:
        acc = acc + bias_ref[...]
    for start, size, kind, scale in segments:
        for b in range(size // LANES):
            lo = start + b * LANES
            xb = acc[:, lo:lo + LANES]
            if kind == "rope128":
                xb = _rotate(xb, tabs[0], tabs[1], tabs[2], ROPE_FREQS)
            elif kind == "rope64":
                xb = _rotate(xb, tabs[0], tabs[1], tabs[2], ROPE_FREQS // 2)
            elif kind == "silu":
                xb = _silu(xb)
            elif kind == "sigmoid":
                xb = jax.nn.sigmoid(xb)
            elif kind == "idx_misc":
                rot = _rotate(xb, tabs[0], tabs[1], tabs[2], ROPE_FREQS // 2)
                lane = lax.broadcasted_iota(jnp.int32, xb.shape, 1)
                xb = jnp.where(lane < IDX_DIM, rot, xb * scale)
            if kind != "idx_misc" and scale != 1.0:
                xb = xb * scale
            o_ref[:, lo:lo + LANES] = xb.astype(o_ref.dtype)


def _project(name, h, w, segments, out_dtype, tabs=(), bias=None, tm=512, tn=None):
    l, d = h.shape
    n = w.shape[1]
    tn = n if tn is None else tn
    in_specs = [pl.BlockSpec((tm, d), lambda i, j: (i, 0)), pl.BlockSpec((d, tn), lambda i, j: (0, j))]
    args = [h, w]
    if bias is not None:
        in_specs.append(pl.BlockSpec((1, tn), lambda i, j: (0, j)))
        args.append(bias.reshape(1, n))
    for t in tabs:
        in_specs.append(pl.BlockSpec((tm, LANES), lambda i, j: (i, 0)))
        args.append(t)
    kern = functools.partial(_proj_kernel, segments=segments, n_tab=len(tabs), has_bias=bias is not None)
    return pl.pallas_call(
        kern,
        grid=(l // tm, n // tn),
        in_specs=in_specs,
        out_specs=pl.BlockSpec((tm, tn), lambda i, j: (i, j)),
        out_shape=jax.ShapeDtypeStruct((l, n), out_dtype),
        compiler_params=_params(("parallel", "arbitrary")),
        name="in_proj_" + name,
    )(*args)


def _nt_dot(a, b):
    return lax.dot_general(a, b, (((1,), (1,)), ((), ())), preferred_element_type=F32)


def _dsa_kernel(q_ref, qi_ref, wt_ref, kidx_ref, k_ref, vt_ref, o_ref,
                key_ref, m_ref, l_ref, acc_ref, *, topk):
    i = pl.program_id(0)
    n_sub = (i + 1) * (Q_TILE // KEY_SUB)
    n_tile = (i + 1) * (Q_TILE // KEY_TILE)
    q_chunk = i * (Q_TILE // CHUNK) + lax.broadcasted_iota(jnp.int32, (1, Q_TILE), 1) // CHUNK

    def score_body(t, carry):
        kt = kidx_ref[pl.ds(pl.multiple_of(t * KEY_SUB, KEY_SUB), KEY_SUB), :]
        acc = jnp.zeros((KEY_SUB, Q_TILE), F32)
        for h in range(IDX_HEADS):
            logit = _nt_dot(kt, qi_ref[:, h * IDX_DIM:(h + 1) * IDX_DIM])
            acc = acc + wt_ref[h:h + 1, :] * jnp.maximum(logit, 0.0)
        bits = lax.bitcast_convert_type(acc, jnp.int32)
        key = bits ^ ((bits >> 31) & jnp.int32(0x7FFFFFFF))
        admissible = (t * (KEY_SUB // CHUNK)) <= q_chunk
        key_ref[pl.ds(pl.multiple_of(t * KEY_SUB, KEY_SUB), KEY_SUB), :] = jnp.where(admissible, key, INT_MIN)
        return carry

    lax.fori_loop(0, n_sub, score_body, 0)

    def bit_body(b, thr):
        cand = thr ^ lax.shift_left(jnp.int32(1), 31 - b)

        def count_body(t, cnt):
            key = key_ref[pl.ds(pl.multiple_of(t * KEY_SUB, KEY_SUB), KEY_SUB), :]
            return cnt + (key >= cand).astype(jnp.int32)

        cnt = lax.fori_loop(0, n_sub, count_body, jnp.zeros((KEY_SUB, Q_TILE), jnp.int32))
        total = jnp.sum(cnt, axis=0, keepdims=True)
        return jnp.where(total >= topk, cand, thr)

    thr = lax.fori_loop(0, 32, bit_body, jnp.full((1, Q_TILE), INT_MIN, jnp.int32))
    thr = jnp.maximum(thr, INT_MIN + 1)

    m_ref[...] = jnp.full(m_ref.shape, NEG_BIG, F32)
    l_ref[...] = jnp.zeros(l_ref.shape, F32)
    acc_ref[...] = jnp.zeros(acc_ref.shape, F32)
    grp = ATT_HEADS // ATT_KV_HEADS

    def att_body(t, carry):
        rows = pl.ds(pl.multiple_of(t * KEY_TILE, KEY_TILE), KEY_TILE)
        bias = jnp.where(key_ref[rows, :] >= thr, 0.0, NEG_BIG)
        for h in range(ATT_HEADS):
            kv = h // grp
            s = _nt_dot(k_ref[rows, kv * HEAD_DIM:(kv + 1) * HEAD_DIM],
                        q_ref[:, h * HEAD_DIM:(h + 1) * HEAD_DIM]) + bias
            m_prev = m_ref[h:h + 1, :]
            m_new = jnp.maximum(m_prev, jnp.max(s, axis=0, keepdims=True))
            alpha = jnp.exp2(m_prev - m_new)
            p = jnp.exp2(s - m_new)
            l_ref[h:h + 1, :] = alpha * l_ref[h:h + 1, :] + jnp.sum(p, axis=0, keepdims=True)
            pv = jnp.dot(vt_ref[t, kv * HEAD_DIM:(kv + 1) * HEAD_DIM, :], p.astype(BF16),
                         preferred_element_type=F32)
            acc_ref[h] = alpha * acc_ref[h] + pv
            m_ref[h:h + 1, :] = m_new
        return carry

    lax.fori_loop(0, n_tile, att_body, 0)
    for h in range(ATT_HEADS):
        out_t = acc_ref[h] / l_ref[h:h + 1, :]
        o_ref[:, h * HEAD_DIM:(h + 1) * HEAD_DIM] = out_t.T.astype(o_ref.dtype)


def _dsa(q, k, v, q_idx, k_idx, w_idx_t, out_dtype):
    l = q.shape[0]
    topk = min(INDEX_TOPK, l // 4)
    n_q = l // Q_TILE
    vt = v.reshape(l // KEY_TILE, KEY_TILE, KV_WIDTH).transpose(0, 2, 1)
    kern = functools.partial(_dsa_kernel, topk=topk)
    return pl.pallas_call(
        kern,
        grid=(n_q,),
        in_specs=[
            pl.BlockSpec((Q_TILE, ATT_WIDTH), lambda i: (i, 0)),
            pl.BlockSpec((Q_TILE, IDX_HEADS * IDX_DIM), lambda i: (i, 0)),
            pl.BlockSpec((IDX_HEADS, Q_TILE), lambda i: (0, i)),
            _resident((l, IDX_DIM)),
            _resident((l, KV_WIDTH)),
            _resident((l // KEY_TILE, KV_WIDTH, KEY_TILE)),
        ],
        out_specs=pl.BlockSpec((Q_TILE, ATT_WIDTH), lambda i: (i, 0)),
        out_shape=jax.ShapeDtypeStruct((l, ATT_WIDTH), out_dtype),
        scratch_shapes=[
            pltpu.VMEM((l, Q_TILE), jnp.int32),
            pltpu.VMEM((ATT_HEADS, Q_TILE), F32),
            pltpu.VMEM((ATT_HEADS, Q_TILE), F32),
            pltpu.VMEM((ATT_HEADS, HEAD_DIM, Q_TILE), F32),
        ],
        compiler_params=_params(("arbitrary",)),
        name="dsa",
    )(q, q_idx, w_idx_t, k_idx, k, vt)


def _s5_kernel(x_ref, r0_ref, per_ref, pei_ref, qre_ref, qim_ref, lam_ref, d_ref, y_ref,
               mt_ref, er_ref, ei_ref, sr_ref, si_ref):
    width = SSM_T * SSM_GROUP_CH
    r0 = r0_ref[0]
    lane = lax.broadcasted_iota(jnp.int32, r0.shape, 1)
    for s in range(SSM_T):
        shifted = r0 if s == 0 else pltpu.roll(r0, s * SSM_GROUP_CH, 1)
        mt_ref[s * SSM_GROUP_CH:(s + 1) * SSM_GROUP_CH, :] = jnp.where(
            lane >= s * SSM_GROUP_CH, shifted, 0.0).astype(BF16)
    x = x_ref[0]
    xb = x.astype(BF16)
    er_ref[...] = jnp.dot(xb, per_ref[0], preferred_element_type=F32)
    ei_ref[...] = jnp.dot(xb, pei_ref[0], preferred_element_type=F32)
    lam_r = lam_ref[0, 0:1, :]
    lam_i = lam_ref[0, 1:2, :]
    n_chunks = x.shape[0]

    def scan_body(c, carry):
        s_r, s_i = carry
        sr_ref[pl.ds(c, 1), :] = s_r
        si_ref[pl.ds(c, 1), :] = s_i
        e_r = er_ref[pl.ds(c, 1), :]
        e_i = ei_ref[pl.ds(c, 1), :]
        return (lam_r * s_r - lam_i * s_i + e_r, lam_r * s_i + lam_i * s_r + e_i)

    zero = jnp.zeros((1, SSM_STATE), F32)
    lax.fori_loop(0, n_chunks, scan_body, (zero, zero))
    y = jnp.dot(xb, mt_ref[...], preferred_element_type=F32)
    y = y + jnp.dot(sr_ref[...].astype(BF16), qre_ref[0], preferred_element_type=F32)
    y = y + jnp.dot(si_ref[...].astype(BF16), qim_ref[0], preferred_element_type=F32)
    y_ref[0] = y + d_ref[0] * x
    del width


def _s5(u, a_re, a_im, log_dt, b_re, b_im, c_re, c_im, d_skip):
    l = u.shape[0]
    t, c, p, g = SSM_T, SSM_GROUP_CH, SSM_STATE, SSM_GROUPS
    n_chunks = l // t
    width = t * c
    dt = jnp.exp(log_dt.astype(F32))[:, None]
    ar, ai = a_re.astype(F32), a_im.astype(F32)
    mag = jnp.exp(ar * dt)
    lb_re, lb_im = mag * jnp.cos(ai * dt), mag * jnp.sin(ai * dt)
    den = ar * ar + ai * ai
    coef_re = ((lb_re - 1.0) * ar + lb_im * ai) / den
    coef_im = (lb_im * ar - (lb_re - 1.0) * ai) / den
    br, bi = b_re.astype(F32), b_im.astype(F32)
    bb = (coef_re[..., None] * br - coef_im[..., None] * bi) + 1j * (coef_re[..., None] * bi + coef_im[..., None] * br)
    lam = lb_re + 1j * lb_im
    cc = c_re.astype(F32) + 1j * c_im.astype(F32)
    pw = jnp.cumprod(jnp.concatenate([jnp.ones((g, 1, p), lam.dtype),
                                      jnp.broadcast_to(lam[:, None, :], (g, t, p))], axis=1), axis=1)
    hp = lax.Precision.HIGHEST
    kj = jnp.einsum("gop,gjp,gpi->gijo", cc, pw[:, :t], bb, precision=hp).real
    r0 = kj.reshape(g, c, width).astype(F32)
    pe = jnp.einsum("gsp,gpi->gsip", pw[:, :t][:, ::-1], bb, precision=hp).reshape(g, width, p)
    qg = jnp.einsum("gop,gjp->gpjo", cc, pw[:, 1:t + 1], precision=hp).reshape(g, p, width)
    lam_t = pw[:, t]
    lam_pack = jnp.stack([lam_t.real, lam_t.imag], axis=1).astype(F32)
    d_tiled = jnp.tile(d_skip.astype(F32), (1, t)).reshape(g, 1, width)
    x = u.reshape(n_chunks, t, g, c).transpose(2, 0, 1, 3).reshape(g, n_chunks, width)
    grp3 = lambda a, b: pl.BlockSpec((1, a, b), lambda i: (i, 0, 0))
    y = pl.pallas_call(
        _s5_kernel,
        grid=(g,),
        in_specs=[grp3(n_chunks, width), grp3(c, width), grp3(width, p), grp3(width, p),
                  grp3(p, width), grp3(p, width), grp3(2, p), grp3(1, width)],
        out_specs=grp3(n_chunks, width),
        out_shape=jax.ShapeDtypeStruct((g, n_chunks, width), F32),
        scratch_shapes=[
            pltpu.VMEM((width, width), BF16),
            pltpu.VMEM((n_chunks, p), F32),
            pltpu.VMEM((n_chunks, p), F32),
            pltpu.VMEM((n_chunks, p), F32),
            pltpu.VMEM((n_chunks, p), F32),
        ],
        compiler_params=_params(("parallel",)),
        name="s5",
    )(x, r0, pe.real.astype(BF16), pe.imag.astype(BF16), qg.real.astype(BF16), (-qg.imag).astype(BF16),
      lam_pack, d_tiled)
    return y.reshape(g, n_chunks, t, c).transpose(1, 2, 0, 3).reshape(l, SSM_WIDTH)


def _gelu_tanh(x):
    return 0.5 * x * (1.0 + jnp.tanh(math.sqrt(2.0 / math.pi) * (x + 0.044715 * (x * x * x))))


def _out_kernel(att_ref, ag_ref, ys_ref, sg_ref, gate_ref, x_ref, watt_ref, wglu_ref, bglu_ref, wssm_ref,
                wout_ref, gfin_ref, o_ref):
    a = (att_ref[...].astype(F32) * ag_ref[...].astype(F32)).astype(BF16)
    y_a = jnp.dot(a, watt_ref[...], preferred_element_type=F32)
    s = _gelu_tanh(ys_ref[...])
    z = jnp.dot(s.astype(BF16), wglu_ref[...], preferred_element_type=F32) + bglu_ref[...]
    s = s * jax.nn.sigmoid(z)
    y_b = jnp.dot((s * sg_ref[...]).astype(BF16), wssm_ref[...], preferred_element_type=F32)
    gates = gate_ref[...].astype(F32)
    mixed = gates[:, :D_MODEL] * y_a + gates[:, D_MODEL:] * y_b
    r = x_ref[...] + jnp.dot(mixed.astype(BF16), wout_ref[...], preferred_element_type=F32)
    y = r * lax.rsqrt(jnp.mean(r * r, axis=-1, keepdims=True) + EPS)
    o_ref[...] = (y * gfin_ref[...]).astype(o_ref.dtype)


def _out_stage(att, att_gate, y_ssm, ssm_gate, gates, x, w_att_out, w_glu, b_glu, w_ssm_out, w_out, g_final, tm=256):
    l = x.shape[0]
    row = lambda n: pl.BlockSpec((tm, n), lambda i: (i, 0))
    return pl.pallas_call(
        _out_kernel,
        grid=(l // tm,),
        in_specs=[row(ATT_WIDTH), row(ATT_WIDTH), row(SSM_WIDTH), row(SSM_WIDTH), row(N_BRANCH * D_MODEL),
                  row(D_MODEL),
                  _resident((ATT_WIDTH, D_MODEL)), _resident((SSM_WIDTH, SSM_WIDTH)), _resident((1, SSM_WIDTH)),
                  _resident((SSM_WIDTH, D_MODEL)), _resident((D_MODEL, D_MODEL)), _resident((1, D_MODEL))],
        out_specs=row(D_MODEL),
        out_shape=jax.ShapeDtypeStruct((l, D_MODEL), x.dtype),
        compiler_params=_params(("parallel",)),
        name="out_stage",
    )(att, att_gate, y_ssm, ssm_gate, gates, x, w_att_out.astype(BF16), w_glu.astype(BF16),
      b_glu.reshape(1, -1), w_ssm_out.astype(BF16), w_out.astype(BF16), g_final.reshape(1, -1))


def _layer(x, positions, g_norm, w_in, b_merge, a_re, a_im, log_dt, ssm_b_re, ssm_b_im, ssm_c_re, ssm_c_im,
           ssm_d, w_glu, b_glu, w_att_out, w_ssm_out, w_out, g_out):
    off = [0]
    for s in SPLITS:
        off.append(off[-1] + s)
    wq, wk, wv, wag, wqi, wki, wwi, wu, wsg, wmg = (w_in[:, off[n]:off[n + 1]].astype(BF16) for n in range(10))
    tabs_a, tabs_b = _rope_tables(positions)
    h = _rmsnorm(x, g_norm, BF16)

    q_scale = HEAD_DIM ** -0.5 * LOG2E
    q = _project("q", h, wq, ((0, ATT_WIDTH, "rope128", q_scale),), BF16, tabs_a)
    kv = _project("kv", h, jnp.concatenate([wk, wv], axis=1),
                  ((0, KV_WIDTH, "rope128", 1.0), (KV_WIDTH, KV_WIDTH, "none", 1.0)), BF16, tabs_a)
    att_gate = _project("att_gate", h, wag, ((0, ATT_WIDTH, "silu", 1.0),), BF16)
    q_idx = _project("q_idx", h, wqi, ((0, IDX_HEADS * IDX_DIM, "rope64", 1.0),), BF16, tabs_b)
    w_misc = jnp.concatenate([wki, wwi, jnp.zeros((D_MODEL, LANES - IDX_DIM - IDX_HEADS), BF16)], axis=1)
    misc = _project("idx_misc", h, w_misc, ((0, LANES, "idx_misc", IDX_HEADS ** -0.5 * IDX_DIM ** -0.5),), F32,
                    tabs_b)
    ug = _project("ssm", h, jnp.concatenate([wu, wsg], axis=1),
                  ((0, SSM_WIDTH, "none", 1.0), (SSM_WIDTH, SSM_WIDTH, "silu", 1.0)), F32)
    gates = _project("merge", h, wmg, ((0, 1024, "sigmoid", 1.0),), BF16, bias=b_merge, tn=1024)

    k_idx = misc[:, :IDX_DIM].astype(BF16)
    w_idx_t = misc[:, IDX_DIM:IDX_DIM + IDX_HEADS].T
    att = _dsa(q, kv[:, :KV_WIDTH], kv[:, KV_WIDTH:], q_idx, k_idx, w_idx_t, BF16)
    y_ssm = _s5(ug[:, :SSM_WIDTH], a_re, a_im, log_dt, ssm_b_re, ssm_b_im, ssm_c_re, ssm_c_im, ssm_d)
    return _out_stage(att, att_gate, y_ssm, ug[:, SSM_WIDTH:], gates, x, w_att_out, w_glu, b_glu, w_ssm_out,
                      w_out, g_out)


def kernel(x, positions, g_norm, w_in, b_merge, a_re, a_im, log_dt, ssm_b_re, ssm_b_im, ssm_c_re, ssm_c_im,
           ssm_d, w_glu, b_glu, w_att_out, w_ssm_out, w_out, g_final):
    depth = w_in.shape[0]
    assert depth == 1 and x.shape[0] == 1, "single layer, single sequence"
    y = _layer(x[0], positions[0], g_norm[0], w_in[0], b_merge[0], a_re[0], a_im[0], log_dt[0], ssm_b_re[0],
               ssm_b_im[0], ssm_c_re[0], ssm_c_im[0], ssm_d[0], w_glu[0], b_glu[0], w_att_out[0], w_ssm_out[0],
               w_out[0], g_final)
    return y[None]
```

```python
import functools
import math

import jax
import jax.numpy as jnp
from jax import lax
from jax.experimental import pallas as pl
from jax.experimental.pallas import tpu as pltpu

D_MODEL = 2048
CHUNK = 64
EPS = 1e-6
ATT_HEADS = 8
ATT_KV_HEADS = 2
HEAD_DIM = 128
ATT_WIDTH = ATT_HEADS * HEAD_DIM
KV_WIDTH = ATT_KV_HEADS * HEAD_DIM
IDX_HEADS = 16
IDX_DIM = 64
INDEX_TOPK = 256
ROPE_THETA = 500000.0
ROPE_FRACTION = 4
SSM_GROUP_CH = 16
SSM_GROUPS = 32
SSM_WIDTH = SSM_GROUPS * SSM_GROUP_CH
SSM_STATE = 64
N_BRANCH = 2
SPLITS = (ATT_WIDTH, KV_WIDTH, KV_WIDTH, ATT_WIDTH, IDX_HEADS * IDX_DIM, IDX_DIM, IDX_HEADS,
          SSM_WIDTH, SSM_WIDTH, N_BRANCH * D_MODEL)

LANES = 128
ROPE_FREQS = HEAD_DIM // ROPE_FRACTION // 2
VMEM_LIMIT = 56 * 1024 * 1024

Q_TILE = 256
KEY_TILE = 256
ATT_CHUNK = 512
SSM_T = 64
LOG2E = 1.4426950408889634
NEG_BIG = -1e30
INT_MIN = -2 ** 31

F32 = jnp.float32
BF16 = jnp.bfloat16


def _params(semantics):
    return pltpu.CompilerParams(dimension_semantics=semantics, vmem_limit_bytes=VMEM_LIMIT)


def _resident(shape):
    zeros = (0,) * len(shape)
    return pl.BlockSpec(shape, lambda *_: zeros, pipeline_mode=pl.Buffered(1))


def _rope_kernel(pos_ref, freq_ref, cos_ref, sin_ref):
    ang = freq_ref[...] * pos_ref[...].astype(F32)
    cos_ref[...] = jnp.cos(ang)
    sin_ref[...] = jnp.sin(ang)


def _rope_tables(positions):
    l = positions.shape[0]
    rd = HEAD_DIM // ROPE_FRACTION
    inv_freq = jnp.exp(-math.log(ROPE_THETA) * jnp.arange(ROPE_FREQS, dtype=F32) * (2.0 / rd))
    cos_t, sin_t = pl.pallas_call(
        _rope_kernel,
        out_shape=(jax.ShapeDtypeStruct((ROPE_FREQS, l), F32),) * 2,
        name="rope_tables",
    )(positions.reshape(1, l), inv_freq.reshape(ROPE_FREQS, 1))
    cos_a, sin_a = cos_t.T, sin_t.T
    one = lambda n: jnp.ones((l, n), F32)
    zero = lambda n: jnp.zeros((l, n), F32)
    c_a = jnp.concatenate([cos_a, cos_a, one(HEAD_DIM - 2 * ROPE_FREQS)], axis=1)
    s1_a = jnp.concatenate([-sin_a, zero(HEAD_DIM - ROPE_FREQS)], axis=1)
    s2_a = jnp.concatenate([zero(ROPE_FREQS), sin_a, zero(HEAD_DIM - 2 * ROPE_FREQS)], axis=1)
    half_b = ROPE_FREQS // 2
    cos_b, sin_b = cos_a[:, 0::2], sin_a[:, 0::2]
    c_b = jnp.concatenate([cos_b, cos_b, one(IDX_DIM - 2 * half_b)], axis=1)
    s1_b = jnp.concatenate([-sin_b, zero(IDX_DIM - half_b)], axis=1)
    s2_b = jnp.concatenate([zero(half_b), sin_b, zero(IDX_DIM - 2 * half_b)], axis=1)
    tile2 = lambda t: jnp.concatenate([t, t], axis=1)
    return (c_a, s1_a, s2_a), (tile2(c_b), tile2(s1_b), tile2(s2_b))


def _rmsnorm_kernel(x_ref, g_ref, o_ref):
    x = x_ref[...]
    y = x * lax.rsqrt(jnp.mean(x * x, axis=-1, keepdims=True) + EPS)
    o_ref[...] = (y * g_ref[...]).astype(o_ref.dtype)


def _rmsnorm(x, g, out_dtype, tm=512):
    l, d = x.shape
    return pl.pallas_call(
        _rmsnorm_kernel,
        grid=(l // tm,),
        in_specs=[pl.BlockSpec((tm, d), lambda i: (i, 0)), pl.BlockSpec((1, d), lambda i: (0, 0))],
        out_specs=pl.BlockSpec((tm, d), lambda i: (i, 0)),
        out_shape=jax.ShapeDtypeStruct((l, d), out_dtype),
        compiler_params=_params(("parallel",)),
        name="rmsnorm",
    )(x, g.reshape(1, d))


def _silu(x):
    return x * jax.nn.sigmoid(x)


def _rotate(xb, c, s1, s2, shift):
    return xb * c + pltpu.roll(xb, LANES - shift, 1) * s1 + pltpu.roll(xb, shift, 1) * s2


def _proj_kernel(*refs, segments, n_tab, has_bias):
    h_ref, w_ref = refs[0], refs[1]
    pos = 2
    bias_ref = None
    if has_bias:
        bias_ref = refs[pos]
        pos += 1
    tabs = [r[...] for r in refs[pos:pos + n_tab]]
    o_ref = refs[pos + n_tab]
    acc = jnp.dot(h_ref[...], w_ref[...], preferred_element_type=F32)
    if has_bias:
        acc = acc + bias_ref[...]
    for start, size, kind, scale in segments:
        for b in range(size // LANES):
            lo = start + b * LANES
            xb = acc[:, lo:lo + LANES]
            if kind == "rope128":
                xb = _rotate(xb, tabs[0], tabs[1], tabs[2], ROPE_FREQS)
            elif kind == "rope64":
                xb = _rotate(xb, tabs[0], tabs[1], tabs[2], ROPE_FREQS // 2)
            elif kind == "silu":
                xb = _silu(xb)
            elif kind == "sigmoid":
                xb = jax.nn.sigmoid(xb)
            elif kind == "idx_misc":
                rot = _rotate(xb, tabs[0], tabs[1], tabs[2], ROPE_FREQS // 2)
                lane = lax.broadcasted_iota(jnp.int32, xb.shape, 1)
                xb = jnp.where(lane < IDX_DIM, rot, xb * scale)
            if kind != "idx_misc" and scale != 1.0:
                xb = xb * scale
            o_ref[:, lo:lo + LANES] = xb.astype(o_ref.dtype)


def _project(name, h, w, segments, out_dtype, tabs=(), bias=None, tm=512, tn=None):
    l, d = h.shape
    n = w.shape[1]
    tn = n if tn is None else tn
    in_specs = [pl.BlockSpec((tm, d), lambda i, j: (i, 0)), pl.BlockSpec((d, tn), lambda i, j: (0, j))]
    args = [h, w]
    if bias is not None:
        in_specs.append(pl.BlockSpec((1, tn), lambda i, j: (0, j)))
        args.append(bias.reshape(1, n))
    for t in tabs:
        in_specs.append(pl.BlockSpec((tm, LANES), lambda i, j: (i, 0)))
        args.append(t)
    kern = functools.partial(_proj_kernel, segments=segments, n_tab=len(tabs), has_bias=bias is not None)
    return pl.pallas_call(
        kern,
        grid=(l // tm, n // tn),
        in_specs=in_specs,
        out_specs=pl.BlockSpec((tm, tn), lambda i, j: (i, j)),
        out_shape=jax.ShapeDtypeStruct((l, n), out_dtype),
        compiler_params=_params(("parallel", "arbitrary")),
        name="in_proj_" + name,
    )(*args)


def _order_key(x):
    bits = lax.bitcast_convert_type(x, jnp.int32)
    return bits ^ ((bits >> 31) & jnp.int32(0x7FFFFFFF))


def _dsa_kernel(qt_ref, qit_ref, wt_ref, kidx_ref, k_ref, vt_ref, o_ref,
                key_ref, s_ref, acc_ref, *, topk, n_att_chunks):
    i = pl.program_id(0)
    n_blk = i + 1
    sub = 8
    grp = ATT_HEADS // ATT_KV_HEADS

    def score_block(t, kmax, diagonal):
        rows = pl.ds(pl.multiple_of(t * KEY_TILE, KEY_TILE), KEY_TILE)
        kt = kidx_ref[rows, :]
        acc = jnp.zeros((KEY_TILE, Q_TILE), F32)
        for h in range(IDX_HEADS):
            logit = jnp.dot(kt, qit_ref[h * IDX_DIM:(h + 1) * IDX_DIM, :], preferred_element_type=F32)
            acc = acc + wt_ref[h:h + 1, :] * jnp.maximum(logit, 0.0)
        key = _order_key(acc)
        if diagonal:
            r_chunk = lax.broadcasted_iota(jnp.int32, key.shape, 0) // CHUNK
            q_chunk = lax.broadcasted_iota(jnp.int32, key.shape, 1) // CHUNK
            key = jnp.where(r_chunk <= q_chunk, key, INT_MIN)
        key_ref[rows, :] = key
        return jnp.maximum(kmax, jnp.max(key.reshape(KEY_TILE // sub, sub, Q_TILE), axis=0))

    kmax = lax.fori_loop(0, i, lambda t, c: score_block(t, c, False),
                         jnp.full((sub, Q_TILE), INT_MIN, jnp.int32))
    kmax = score_block(i, kmax, True)
    kmax = jnp.max(kmax, axis=0, keepdims=True)

    blk_per_it = 2 * ATT_CHUNK // KEY_TILE
    n_it = (n_blk + blk_per_it - 1) // blk_per_it

    def fill_body(t, carry):
        key_ref[pl.ds(pl.multiple_of(t * KEY_TILE, KEY_TILE), KEY_TILE), :] = jnp.full(
            (KEY_TILE, Q_TILE), INT_MIN, jnp.int32)
        return carry

    lax.fori_loop(n_blk, n_it * blk_per_it, fill_body, 0)

    def count_ge(cand):
        cand8 = jnp.broadcast_to(cand, (sub, Q_TILE))

        def count_body(t, accs):
            accs = list(accs)
            base = pl.multiple_of(t * KEY_TILE, KEY_TILE)
            for j in range(KEY_TILE // sub):
                k8 = key_ref[pl.ds(base + j * sub, sub), :]
                accs[j % 4] = accs[j % 4] + (k8 >= cand8).astype(jnp.int32)
            return tuple(accs)

        zero = jnp.zeros((sub, Q_TILE), jnp.int32)
        a0, a1, a2, a3 = lax.fori_loop(0, n_blk, count_body, (zero, zero, zero, zero))
        return jnp.sum((a0 + a1) + (a2 + a3), axis=0, keepdims=True)

    lane_chunk = lax.broadcasted_iota(jnp.int32, (1, Q_TILE), 1) // CHUNK
    n_adm = (i * (Q_TILE // CHUNK) + lane_chunk + 1) * CHUNK
    lo0 = jnp.full((1, Q_TILE), INT_MIN + 1, jnp.int32)
    hi0 = kmax + 1

    def unresolved(lo, hi, cnt_lo):
        return jnp.max(((cnt_lo > topk) & ((hi - lo) != 1)).astype(jnp.int32))

    def bis_cond(carry):
        return carry[0] > 0

    def bis_body(carry):
        _, lo, hi, cnt_lo = carry
        mid = lo + lax.shift_right_logical(hi - lo, 1)
        cnt = count_ge(mid)
        ge = cnt >= topk
        lo, hi, cnt_lo = jnp.where(ge, mid, lo), jnp.where(ge, hi, mid), jnp.where(ge, cnt, cnt_lo)
        return unresolved(lo, hi, cnt_lo), lo, hi, cnt_lo

    _, thr, _, _ = lax.while_loop(bis_cond, bis_body, (unresolved(lo0, hi0, n_adm), lo0, hi0, n_adm))

    last_chunk = n_att_chunks - 1

    def qk(h, c):
        rows = pl.ds(pl.multiple_of(c * ATT_CHUNK, ATT_CHUNK), ATT_CHUNK)
        return jnp.dot(k_ref[h // grp, rows, :], qt_ref[h * HEAD_DIM:(h + 1) * HEAD_DIM, :],
                       preferred_element_type=F32)

    for h in range(ATT_HEADS):
        kv = h // grp
        acc_ref[...] = jnp.zeros(acc_ref.shape, F32)
        s_ref[0] = qk(h, 0)

        def att_body(it, carry, h=h, kv=kv):
            m, l = carry
            for si in range(2):
                c = it * 2 + si
                s = s_ref[si]
                s_ref[1 - si] = qk(h, jnp.minimum(c + 1, last_chunk))
                rows = pl.ds(pl.multiple_of(c * ATT_CHUNK, ATT_CHUNK), ATT_CHUNK)
                s = jnp.where(key_ref[rows, :] >= thr, s, NEG_BIG)
                m_new = jnp.maximum(m, jnp.max(s, axis=0, keepdims=True))
                alpha = jnp.exp2(m - m_new)
                p = jnp.exp2(s - m_new)
                l = alpha * l + jnp.sum(p, axis=0, keepdims=True)
                pv = jnp.dot(vt_ref[c, kv * HEAD_DIM:(kv + 1) * HEAD_DIM, :], p.astype(BF16),
                             preferred_element_type=F32)
                acc_ref[...] = alpha * acc_ref[...] + pv
                m = m_new
            return m, l

        m, l = lax.fori_loop(0, n_it, att_body,
                             (jnp.full((1, Q_TILE), NEG_BIG, F32), jnp.zeros((1, Q_TILE), F32)))
        o_ref[:, h * HEAD_DIM:(h + 1) * HEAD_DIM] = (acc_ref[...] / l).T.astype(o_ref.dtype)


def _dsa(q, k, v, q_idx, k_idx, w_idx_t, out_dtype):
    l = q.shape[0]
    topk = min(INDEX_TOPK, l // 4)
    assert l % (2 * ATT_CHUNK) == 0
    n_att_chunks = l // ATT_CHUNK
    k3 = k.reshape(l, ATT_KV_HEADS, HEAD_DIM).transpose(1, 0, 2)
    vt = v.reshape(n_att_chunks, ATT_CHUNK, KV_WIDTH).transpose(0, 2, 1)
    kern = functools.partial(_dsa_kernel, topk=topk, n_att_chunks=n_att_chunks)
    return pl.pallas_call(
        kern,
        grid=(l // Q_TILE,),
        in_specs=[
            pl.BlockSpec((ATT_WIDTH, Q_TILE), lambda i: (0, i)),
            pl.BlockSpec((IDX_HEADS * IDX_DIM, Q_TILE), lambda i: (0, i)),
            pl.BlockSpec((IDX_HEADS, Q_TILE), lambda i: (0, i)),
            _resident((l, IDX_DIM)),
            _resident((ATT_KV_HEADS, l, HEAD_DIM)),
            _resident((n_att_chunks, KV_WIDTH, ATT_CHUNK)),
        ],
        out_specs=pl.BlockSpec((Q_TILE, ATT_WIDTH), lambda i: (i, 0)),
        out_shape=jax.ShapeDtypeStruct((l, ATT_WIDTH), out_dtype),
        scratch_shapes=[
            pltpu.VMEM((l, Q_TILE), jnp.int32),
            pltpu.VMEM((2, ATT_CHUNK, Q_TILE), F32),
            pltpu.VMEM((HEAD_DIM, Q_TILE), F32),
        ],
        compiler_params=_params(("arbitrary",)),
        name="dsa",
    )(q.T, q_idx.T, w_idx_t, k_idx, k3, vt)


def _s5_kernel(x_ref, r0_ref, per_ref, pei_ref, qre_ref, qim_ref, lam_ref, d_ref, y_ref,
               mt_ref, er_ref, ei_ref, sr_ref, si_ref):
    width = SSM_T * SSM_GROUP_CH
    r0 = r0_ref[0]
    lane = lax.broadcasted_iota(jnp.int32, r0.shape, 1)
    for s in range(SSM_T):
        shifted = r0 if s == 0 else pltpu.roll(r0, s * SSM_GROUP_CH, 1)
        mt_ref[s * SSM_GROUP_CH:(s + 1) * SSM_GROUP_CH, :] = jnp.where(
            lane >= s * SSM_GROUP_CH, shifted, 0.0).astype(BF16)
    x = x_ref[0]
    xb = x.astype(BF16)
    er_ref[...] = jnp.dot(xb, per_ref[0], preferred_element_type=F32)
    ei_ref[...] = jnp.dot(xb, pei_ref[0], preferred_element_type=F32)
    lam_r = lam_ref[0, 0:1, :]
    lam_i = lam_ref[0, 1:2, :]
    n_chunks = x.shape[0]

    def scan_body(c, carry):
        s_r, s_i = carry
        sr_ref[pl.ds(c, 1), :] = s_r
        si_ref[pl.ds(c, 1), :] = s_i
        e_r = er_ref[pl.ds(c, 1), :]
        e_i = ei_ref[pl.ds(c, 1), :]
        return (lam_r * s_r - lam_i * s_i + e_r, lam_r * s_i + lam_i * s_r + e_i)

    zero = jnp.zeros((1, SSM_STATE), F32)
    lax.fori_loop(0, n_chunks, scan_body, (zero, zero))
    y = jnp.dot(xb, mt_ref[...], preferred_element_type=F32)
    y = y + jnp.dot(sr_ref[...].astype(BF16), qre_ref[0], preferred_element_type=F32)
    y = y + jnp.dot(si_ref[...].astype(BF16), qim_ref[0], preferred_element_type=F32)
    y_ref[0] = y + d_ref[0] * x
    del width


def _s5(u, a_re, a_im, log_dt, b_re, b_im, c_re, c_im, d_skip):
    l = u.shape[0]
    t, c, p, g = SSM_T, SSM_GROUP_CH, SSM_STATE, SSM_GROUPS
    n_chunks = l // t
    width = t * c
    dt = jnp.exp(log_dt.astype(F32))[:, None]
    ar, ai = a_re.astype(F32), a_im.astype(F32)
    mag = jnp.exp(ar * dt)
    lb_re, lb_im = mag * jnp.cos(ai * dt), mag * jnp.sin(ai * dt)
    den = ar * ar + ai * ai
    coef_re = ((lb_re - 1.0) * ar + lb_im * ai) / den
    coef_im = (lb_im * ar - (lb_re - 1.0) * ai) / den
    br, bi = b_re.astype(F32), b_im.astype(F32)
    bb_re = coef_re[..., None] * br - coef_im[..., None] * bi
    bb_im = coef_re[..., None] * bi + coef_im[..., None] * br
    cr, ci = c_re.astype(F32), c_im.astype(F32)
    pr, pi = jnp.ones((g, 1, p), F32), jnp.zeros((g, 1, p), F32)
    sr, si = lb_re[:, None, :], lb_im[:, None, :]
    while pr.shape[1] < t + 1:
        pr, pi = (jnp.concatenate([pr, pr * sr - pi * si], axis=1),
                  jnp.concatenate([pi, pr * si + pi * sr], axis=1))
        sr, si = sr * sr - si * si, 2.0 * sr * si
    pr, pi = pr[:, :t + 1], pi[:, :t + 1]
    gc_re = cr[:, :, None, :] * pr[:, None] - ci[:, :, None, :] * pi[:, None]
    gc_im = cr[:, :, None, :] * pi[:, None] + ci[:, :, None, :] * pr[:, None]
    hp = lax.Precision.HIGHEST
    kj = (jnp.einsum("gojp,gpi->gijo", gc_re[:, :, :t], bb_re, precision=hp)
          - jnp.einsum("gojp,gpi->gijo", gc_im[:, :, :t], bb_im, precision=hp))
    r0 = kj.reshape(g, c, width)
    rr, ri = pr[:, :t][:, ::-1][:, :, None, :], pi[:, :t][:, ::-1][:, :, None, :]
    tb_re, tb_im = bb_re.transpose(0, 2, 1)[:, None], bb_im.transpose(0, 2, 1)[:, None]
    pe_re = (rr * tb_re - ri * tb_im).reshape(g, width, p)
    pe_im = (rr * tb_im + ri * tb_re).reshape(g, width, p)
    q_re = gc_re[:, :, 1:t + 1].transpose(0, 3, 2, 1).reshape(g, p, width)
    q_im = gc_im[:, :, 1:t + 1].transpose(0, 3, 2, 1).reshape(g, p, width)
    lam_pack = jnp.stack([pr[:, t], pi[:, t]], axis=1)
    d_tiled = jnp.tile(d_skip.astype(F32), (1, t)).reshape(g, 1, width)
    x = u.reshape(n_chunks, t, g, c).transpose(2, 0, 1, 3).reshape(g, n_chunks, width)
    grp3 = lambda a, b: pl.BlockSpec((1, a, b), lambda i: (i, 0, 0))
    y = pl.pallas_call(
        _s5_kernel,
        grid=(g,),
        in_specs=[grp3(n_chunks, width), grp3(c, width), grp3(width, p), grp3(width, p),
                  grp3(p, width), grp3(p, width), grp3(2, p), grp3(1, width)],
        out_specs=grp3(n_chunks, width),
        out_shape=jax.ShapeDtypeStruct((g, n_chunks, width), F32),
        scratch_shapes=[
            pltpu.VMEM((width, width), BF16),
            pltpu.VMEM((n_chunks, p), F32),
            pltpu.VMEM((n_chunks, p), F32),
            pltpu.VMEM((n_chunks, p), F32),
            pltpu.VMEM((n_chunks, p), F32),
        ],
        compiler_params=_params(("parallel",)),
        name="s5",
    )(x, r0, pe_re.astype(BF16), pe_im.astype(BF16), q_re.astype(BF16), (-q_im).astype(BF16),
      lam_pack, d_tiled)
    return y.reshape(g, n_chunks, t, c).transpose(1, 2, 0, 3).reshape(l, SSM_WIDTH)


def _gelu_tanh(x):
    return 0.5 * x * (1.0 + jnp.tanh(math.sqrt(2.0 / math.pi) * (x + 0.044715 * (x * x * x))))


def _out_kernel(att_ref, ag_ref, ys_ref, sg_ref, gate_ref, x_ref, watt_ref, wglu_ref, bglu_ref, wssm_ref,
                wout_ref, gfin_ref, o_ref):
    a = (att_ref[...].astype(F32) * ag_ref[...].astype(F32)).astype(BF16)
    y_a = jnp.dot(a, watt_ref[...], preferred_element_type=F32)
    s = _gelu_tanh(ys_ref[...])
    z = jnp.dot(s.astype(BF16), wglu_ref[...], preferred_element_type=F32) + bglu_ref[...]
    s = s * jax.nn.sigmoid(z)
    y_b = jnp.dot((s * sg_ref[...]).astype(BF16), wssm_ref[...], preferred_element_type=F32)
    gates = gate_ref[...].astype(F32)
    mixed = gates[:, :D_MODEL] * y_a + gates[:, D_MODEL:] * y_b
    r = x_ref[...] + jnp.dot(mixed.astype(BF16), wout_ref[...], preferred_element_type=F32)
    y = r * lax.rsqrt(jnp.mean(r * r, axis=-1, keepdims=True) + EPS)
    o_ref[...] = (y * gfin_ref[...]).astype(o_ref.dtype)


def _out_stage(att, att_gate, y_ssm, ssm_gate, gates, x, w_att_out, w_glu, b_glu, w_ssm_out, w_out, g_final, tm=256):
    l = x.shape[0]
    row = lambda n: pl.BlockSpec((tm, n), lambda i: (i, 0))
    return pl.pallas_call(
        _out_kernel,
        grid=(l // tm,),
        in_specs=[row(ATT_WIDTH), row(ATT_WIDTH), row(SSM_WIDTH), row(SSM_WIDTH), row(N_BRANCH * D_MODEL),
                  row(D_MODEL),
                  _resident((ATT_WIDTH, D_MODEL)), _resident((SSM_WIDTH, SSM_WIDTH)), _resident((1, SSM_WIDTH)),
                  _resident((SSM_WIDTH, D_MODEL)), _resident((D_MODEL, D_MODEL)), _resident((1, D_MODEL))],
        out_specs=row(D_MODEL),
        out_shape=jax.ShapeDtypeStruct((l, D_MODEL), x.dtype),
        compiler_params=_params(("parallel",)),
        name="out_stage",
    )(att, att_gate, y_ssm, ssm_gate, gates, x, w_att_out.astype(BF16), w_glu.astype(BF16),
      b_glu.reshape(1, -1), w_ssm_out.astype(BF16), w_out.astype(BF16), g_final.reshape(1, -1))


def _layer(x, positions, g_norm, w_in, b_merge, a_re, a_im, log_dt, ssm_b_re, ssm_b_im, ssm_c_re, ssm_c_im,
           ssm_d, w_glu, b_glu, w_att_out, w_ssm_out, w_out, g_out):
    off = [0]
    for s in SPLITS:
        off.append(off[-1] + s)
    wq, wk, wv, wag, wqi, wki, wwi, wu, wsg, wmg = (w_in[:, off[n]:off[n + 1]].astype(BF16) for n in range(10))
    tabs_a, tabs_b = _rope_tables(positions)
    h = _rmsnorm(x, g_norm, BF16)

    q_scale = HEAD_DIM ** -0.5 * LOG2E
    q = _project("q", h, wq, ((0, ATT_WIDTH, "rope128", q_scale),), BF16, tabs_a)
    kv = _project("kv", h, jnp.concatenate([wk, wv], axis=1),
                  ((0, KV_WIDTH, "rope128", 1.0), (KV_WIDTH, KV_WIDTH, "none", 1.0)), BF16, tabs_a)
    att_gate = _project("att_gate", h, wag, ((0, ATT_WIDTH, "silu", 1.0),), BF16)
    q_idx = _project("q_idx", h, wqi, ((0, IDX_HEADS * IDX_DIM, "rope64", 1.0),), BF16, tabs_b)
    w_misc = jnp.concatenate([wki, wwi, jnp.zeros((D_MODEL, LANES - IDX_DIM - IDX_HEADS), BF16)], axis=1)
    misc = _project("idx_misc", h, w_misc, ((0, LANES, "idx_misc", IDX_HEADS ** -0.5 * IDX_DIM ** -0.5),), F32,
                    tabs_b)
    ug = _project("ssm", h, jnp.concatenate([wu, wsg], axis=1),
                  ((0, SSM_WIDTH, "none", 1.0), (SSM_WIDTH, SSM_WIDTH, "silu", 1.0)), F32)
    gates = _project("merge", h, wmg, ((0, 1024, "sigmoid", 1.0),), BF16, bias=b_merge, tn=1024)

    k_idx = misc[:, :IDX_DIM].astype(BF16)
    w_idx_t = misc[:, IDX_DIM:IDX_DIM + IDX_HEADS].T
    att = _dsa(q, kv[:, :KV_WIDTH], kv[:, KV_WIDTH:], q_idx, k_idx, w_idx_t, BF16)
    y_ssm = _s5(ug[:, :SSM_WIDTH], a_re, a_im, log_dt, ssm_b_re, ssm_b_im, ssm_c_re, ssm_c_im, ssm_d)
    return _out_stage(att, att_gate, y_ssm, ug[:, SSM_WIDTH:], gates, x, w_att_out, w_glu, b_glu, w_ssm_out,
                      w_out, g_out)


def kernel(x, positions, g_norm, w_in, b_merge, a_re, a_im, log_dt, ssm_b_re, ssm_b_im, ssm_c_re, ssm_c_im,
           ssm_d, w_glu, b_glu, w_att_out, w_ssm_out, w_out, g_final):
    depth = w_in.shape[0]
    assert depth == 1 and x.shape[0] == 1, "single layer, single sequence"
    y = _layer(x[0], positions[0], g_norm[0], w_in[0], b_merge[0], a_re[0], a_im[0], log_dt[0], ssm_b_re[0],
               ssm_b_im[0], ssm_c_re[0], ssm_c_im[0], ssm_d[0], w_glu[0], b_glu[0], w_att_out[0], w_ssm_out[0],
               w_out[0], g_final)
    return y[None]
```

```python
import functools
import math

import jax
import jax.numpy as jnp
from jax import lax
from jax.experimental import pallas as pl
from jax.experimental.pallas import tpu as pltpu

D_MODEL = 2048
CHUNK = 64
EPS = 1e-6
ATT_HEADS = 8
ATT_KV_HEADS = 2
HEAD_DIM = 128
ATT_WIDTH = ATT_HEADS * HEAD_DIM
KV_WIDTH = ATT_KV_HEADS * HEAD_DIM
IDX_HEADS = 16
IDX_DIM = 64
INDEX_TOPK = 256
ROPE_THETA = 500000.0
ROPE_FRACTION = 4
SSM_GROUP_CH = 16
SSM_GROUPS = 32
SSM_WIDTH = SSM_GROUPS * SSM_GROUP_CH
SSM_STATE = 64
N_BRANCH = 2
SPLITS = (ATT_WIDTH, KV_WIDTH, KV_WIDTH, ATT_WIDTH, IDX_HEADS * IDX_DIM, IDX_DIM, IDX_HEADS,
          SSM_WIDTH, SSM_WIDTH, N_BRANCH * D_MODEL)

LANES = 128
ROPE_FREQS = HEAD_DIM // ROPE_FRACTION // 2
VMEM_LIMIT = 56 * 1024 * 1024

Q_TILE = 256
KEY_TILE = 256
ATT_CHUNK = 512
SSM_T = 64
LOG2E = 1.4426950408889634
NEG_BIG = -1e30

F32 = jnp.float32
BF16 = jnp.bfloat16


def _params(semantics):
    return pltpu.CompilerParams(dimension_semantics=semantics, vmem_limit_bytes=VMEM_LIMIT)


def _resident(shape):
    zeros = (0,) * len(shape)
    return pl.BlockSpec(shape, lambda *_: zeros, pipeline_mode=pl.Buffered(1))


def _rope_kernel(pos_ref, freq_ref, cos_ref, sin_ref):
    ang = freq_ref[...] * pos_ref[...].astype(F32)
    cos_ref[...] = jnp.cos(ang)
    sin_ref[...] = jnp.sin(ang)


def _rope_tables(positions):
    l = positions.shape[0]
    rd = HEAD_DIM // ROPE_FRACTION
    inv_freq = jnp.exp(-math.log(ROPE_THETA) * jnp.arange(ROPE_FREQS, dtype=F32) * (2.0 / rd))
    cos_t, sin_t = pl.pallas_call(
        _rope_kernel,
        out_shape=(jax.ShapeDtypeStruct((ROPE_FREQS, l), F32),) * 2,
        name="rope_tables",
    )(positions.reshape(1, l), inv_freq.reshape(ROPE_FREQS, 1))
    return cos_t.T, sin_t.T


def _rmsnorm_kernel(x_ref, g_ref, o_ref):
    x = x_ref[...]
    y = x * lax.rsqrt(jnp.mean(x * x, axis=-1, keepdims=True) + EPS)
    o_ref[...] = (y * g_ref[...]).astype(o_ref.dtype)


def _rmsnorm(x, g, out_dtype, tm=512):
    l, d = x.shape
    return pl.pallas_call(
        _rmsnorm_kernel,
        grid=(l // tm,),
        in_specs=[pl.BlockSpec((tm, d), lambda i: (i, 0)), pl.BlockSpec((1, d), lambda i: (0, 0))],
        out_specs=pl.BlockSpec((tm, d), lambda i: (i, 0)),
        out_shape=jax.ShapeDtypeStruct((l, d), out_dtype),
        compiler_params=_params(("parallel",)),
        name="rmsnorm",
    )(x, g.reshape(1, d))


def _silu(x):
    return x * jax.nn.sigmoid(x)


def _rope_lane_tables(cos, sin, half, period, stride):
    j = lax.broadcasted_iota(jnp.int32, (ROPE_FREQS, LANES), 0)
    lane = lax.broadcasted_iota(jnp.int32, (ROPE_FREQS, LANES), 1) % period
    used = (j % stride) == 0
    first = (used & (lane == j // stride)).astype(F32)
    second = (used & (lane == j // stride + half)).astype(F32)
    lane1 = lax.broadcasted_iota(jnp.int32, (1, LANES), 1) % period
    expand = functools.partial(jnp.dot, precision=lax.Precision.HIGHEST, preferred_element_type=F32)
    c = expand(cos, first + second) + (lane1 >= 2 * half).astype(F32)
    return c, expand(sin, -first), expand(sin, second)


def _rotate(xb, tabs, half):
    c, s1, s2 = tabs
    return xb * c + pltpu.roll(xb, LANES - half, 1) * s1 + pltpu.roll(xb, half, 1) * s2


def _proj_kernel(*refs, segments, has_rope, has_bias):
    h_ref, w_ref = refs[0], refs[1]
    pos = 2
    if has_bias:
        bias_ref = refs[pos]
        pos += 1
    if has_rope:
        cos, sin = refs[pos][...], refs[pos + 1][...]
        pos += 2
    o_ref = refs[pos]
    acc = jnp.dot(h_ref[...], w_ref[...], preferred_element_type=F32)
    if has_bias:
        acc = acc + bias_ref[...]
    kinds = {kind for _, _, kind, _ in segments}
    if "rope128" in kinds:
        tabs128 = _rope_lane_tables(cos, sin, ROPE_FREQS, HEAD_DIM, 1)
    if kinds & {"rope64", "idx_misc"}:
        tabs64 = _rope_lane_tables(cos, sin, ROPE_FREQS // 2, IDX_DIM, 2)
    for start, size, kind, scale in segments:
        for b in range(size // LANES):
            lo = start + b * LANES
            xb = acc[:, lo:lo + LANES]
            if kind == "rope128":
                xb = _rotate(xb, tabs128, ROPE_FREQS)
            elif kind == "rope64":
                xb = _rotate(xb, tabs64, ROPE_FREQS // 2)
            elif kind == "silu":
                xb = _silu(xb)
            elif kind == "sigmoid":
                xb = jax.nn.sigmoid(xb)
            elif kind == "idx_misc":
                lane = lax.broadcasted_iota(jnp.int32, xb.shape, 1)
                xb = jnp.where(lane < IDX_DIM, _rotate(xb, tabs64, ROPE_FREQS // 2), xb * scale)
            if kind != "idx_misc" and scale != 1.0:
                xb = xb * scale
            o_ref[:, lo:lo + LANES] = xb.astype(o_ref.dtype)


def _project(name, h, w, col0, n, segments, out_dtype, rope=None, bias=None, tm=512, tn=None):
    l, d = h.shape
    tn = n if tn is None else tn
    assert col0 % tn == 0 and n % tn == 0
    jb = col0 // tn
    in_specs = [pl.BlockSpec((tm, d), lambda i, j: (i, 0)), pl.BlockSpec((d, tn), lambda i, j: (0, jb + j))]
    args = [h, w]
    if bias is not None:
        in_specs.append(pl.BlockSpec((1, tn), lambda i, j: (0, j)))
        args.append(bias.reshape(1, n))
    if rope is not None:
        in_specs += [pl.BlockSpec((tm, ROPE_FREQS), lambda i, j: (i, 0))] * 2
        args += list(rope)
    kern = functools.partial(_proj_kernel, segments=segments, has_rope=rope is not None, has_bias=bias is not None)
    return pl.pallas_call(
        kern,
        grid=(l // tm, n // tn),
        in_specs=in_specs,
        out_specs=pl.BlockSpec((tm, tn), lambda i, j: (i, j)),
        out_shape=jax.ShapeDtypeStruct((l, n), out_dtype),
        compiler_params=_params(("parallel", "arbitrary")),
        name="in_proj_" + name,
    )(*args)


def _col_reduce(x, op):
    r, q = x.shape
    y = op(x.reshape(r // 64, 64, q), axis=0) if r > 64 else x
    y = op(y.reshape(8, 8, q), axis=0)
    return op(y, axis=0, keepdims=True)


def _dsa_kernel(qt_ref, qit_ref, wt_ref, kidx_ref, k_ref, vt_ref, o_ref,
                sc_ref, s_ref, acc_ref, *, topk, n_att_chunks):
    i = pl.program_id(0)
    n_blk = i + 1
    sub = 8
    grp = ATT_HEADS // ATT_KV_HEADS
    inf = float("inf")

    def score_block(t, carry, diagonal):
        smin, smax = carry
        rows = pl.ds(pl.multiple_of(t * KEY_TILE, KEY_TILE), KEY_TILE)
        kt = kidx_ref[rows, :]
        acc = jnp.zeros((KEY_TILE, Q_TILE), F32)
        for h in range(IDX_HEADS):
            logit = jnp.dot(kt, qit_ref[h * IDX_DIM:(h + 1) * IDX_DIM, :], preferred_element_type=F32)
            acc = acc + wt_ref[h:h + 1, :] * jnp.maximum(logit, 0.0)
        lo_src = acc
        if diagonal:
            r_chunk = lax.broadcasted_iota(jnp.int32, acc.shape, 0) // CHUNK
            q_chunk = lax.broadcasted_iota(jnp.int32, acc.shape, 1) // CHUNK
            lo_src = jnp.where(r_chunk <= q_chunk, acc, inf)
            acc = jnp.where(r_chunk <= q_chunk, acc, -inf)
        sc_ref[rows, :] = acc
        return (jnp.minimum(smin, jnp.min(lo_src.reshape(KEY_TILE // sub, sub, Q_TILE), axis=0)),
                jnp.maximum(smax, jnp.max(acc.reshape(KEY_TILE // sub, sub, Q_TILE), axis=0)))

    carry = lax.fori_loop(0, i, lambda t, c: score_block(t, c, False),
                          (jnp.full((sub, Q_TILE), inf, F32), jnp.full((sub, Q_TILE), -inf, F32)))
    smin, smax = score_block(i, carry, True)
    smin = jnp.min(smin, axis=0, keepdims=True)
    smax = jnp.max(smax, axis=0, keepdims=True)

    blk_per_it = 2 * ATT_CHUNK // KEY_TILE
    n_it = (n_blk + blk_per_it - 1) // blk_per_it

    def fill_body(t, carry):
        sc_ref[pl.ds(pl.multiple_of(t * KEY_TILE, KEY_TILE), KEY_TILE), :] = jnp.full(
            (KEY_TILE, Q_TILE), -inf, F32)
        return carry

    lax.fori_loop(n_blk, n_it * blk_per_it, fill_body, 0)

    def count_ge(cand):
        def count_body(t, accs):
            blk = sc_ref[pl.ds(pl.multiple_of(t * KEY_TILE, KEY_TILE), KEY_TILE), :]
            hit = (blk >= cand).astype(jnp.int32)
            return accs + jnp.sum(hit.reshape(4, KEY_TILE // (4 * sub), sub, Q_TILE), axis=1)

        accs = lax.fori_loop(0, n_blk, count_body, jnp.zeros((4, sub, Q_TILE), jnp.int32))
        return jnp.sum(jnp.sum(accs, axis=0), axis=0, keepdims=True)

    lane_chunk = lax.broadcasted_iota(jnp.int32, (1, Q_TILE), 1) // CHUNK
    n_adm = (i * (Q_TILE // CHUNK) + lane_chunk + 1) * CHUNK
    hi0 = smax + 0.5 * (smax - smin) + 1e-30

    def bis_body(carry):
        _, lo, hi, cnt_lo, live_i = carry
        mid = lo + 0.5 * (hi - lo)
        live = (live_i > 0) & (mid > lo) & (mid < hi)
        cnt = count_ge(mid)
        ge = cnt >= topk
        lo = jnp.where(live & ge, mid, lo)
        hi = jnp.where(live & ~ge, mid, hi)
        cnt_lo = jnp.where(live & ge, cnt, cnt_lo)
        live_i = (live & (cnt_lo > topk)).astype(jnp.int32)
        return jnp.max(live_i), lo, hi, cnt_lo, live_i

    def bis_cond(carry):
        return carry[0] > 0

    live0 = (n_adm > topk).astype(jnp.int32)
    _, thr, _, _, _ = lax.while_loop(bis_cond, bis_body, (jnp.max(live0), smin, hi0, n_adm, live0))

    last_chunk = n_att_chunks - 1

    def qk(h, c):
        rows = pl.ds(pl.multiple_of(c * ATT_CHUNK, ATT_CHUNK), ATT_CHUNK)
        return jnp.dot(k_ref[h // grp, rows, :], qt_ref[h * HEAD_DIM:(h + 1) * HEAD_DIM, :],
                       preferred_element_type=F32)

    for kv in range(ATT_KV_HEADS):
        heads = range(kv * grp, (kv + 1) * grp)
        acc_ref[...] = jnp.zeros(acc_ref.shape, F32)
        for g, h in enumerate(heads):
            s_ref[0, g] = qk(h, 0)

        def att_body(it, carry, kv=kv, heads=heads):
            ms, ls = list(carry[0]), list(carry[1])
            for si in range(2):
                c = it * 2 + si
                rows = pl.ds(pl.multiple_of(c * ATT_CHUNK, ATT_CHUNK), ATT_CHUNK)
                bias = jnp.where(sc_ref[rows, :] >= thr, 0.0, NEG_BIG)
                v_t = vt_ref[c, kv * HEAD_DIM:(kv + 1) * HEAD_DIM, :]
                for g, h in enumerate(heads):
                    s = s_ref[si, g] + bias
                    s_ref[1 - si, g] = qk(h, jnp.minimum(c + 1, last_chunk))
                    m_new = jnp.maximum(ms[g], _col_reduce(s, jnp.max))
                    alpha = jnp.exp2(ms[g] - m_new)
                    p = jnp.exp2(s - m_new)
                    ls[g] = alpha * ls[g] + _col_reduce(p, jnp.sum)
                    pv = jnp.dot(v_t, p.astype(BF16), preferred_element_type=F32)
                    acc_ref[g] = alpha * acc_ref[g] + pv
                    ms[g] = m_new
            return tuple(ms), tuple(ls)

        m0 = tuple(jnp.full((1, Q_TILE), NEG_BIG, F32) for _ in heads)
        l0 = tuple(jnp.zeros((1, Q_TILE), F32) for _ in heads)
        _, ls = lax.fori_loop(0, n_it, att_body, (m0, l0))
        for g, h in enumerate(heads):
            o_ref[:, h * HEAD_DIM:(h + 1) * HEAD_DIM] = (acc_ref[g] / ls[g]).T.astype(o_ref.dtype)


def _dsa(q, k, v, q_idx, k_idx, w_idx_t, out_dtype):
    l = q.shape[0]
    topk = min(INDEX_TOPK, l // 4)
    assert l % (2 * ATT_CHUNK) == 0
    n_att_chunks = l // ATT_CHUNK
    k3 = k.reshape(l, ATT_KV_HEADS, HEAD_DIM).transpose(1, 0, 2)
    vt = v.reshape(n_att_chunks, ATT_CHUNK, KV_WIDTH).transpose(0, 2, 1)
    kern = functools.partial(_dsa_kernel, topk=topk, n_att_chunks=n_att_chunks)
    return pl.pallas_call(
        kern,
        grid=(l // Q_TILE,),
        in_specs=[
            pl.BlockSpec((ATT_WIDTH, Q_TILE), lambda i: (0, i)),
            pl.BlockSpec((IDX_HEADS * IDX_DIM, Q_TILE), lambda i: (0, i)),
            pl.BlockSpec((IDX_HEADS, Q_TILE), lambda i: (0, i)),
            _resident((l, IDX_DIM)),
            _resident((ATT_KV_HEADS, l, HEAD_DIM)),
            _resident((n_att_chunks, KV_WIDTH, ATT_CHUNK)),
        ],
        out_specs=pl.BlockSpec((Q_TILE, ATT_WIDTH), lambda i: (i, 0)),
        out_shape=jax.ShapeDtypeStruct((l, ATT_WIDTH), out_dtype),
        scratch_shapes=[
            pltpu.VMEM((l, Q_TILE), F32),
            pltpu.VMEM((2, ATT_HEADS // ATT_KV_HEADS, ATT_CHUNK, Q_TILE), F32),
            pltpu.VMEM((ATT_HEADS // ATT_KV_HEADS, HEAD_DIM, Q_TILE), F32),
        ],
        compiler_params=_params(("arbitrary",)),
        name="dsa",
    )(q.T, q_idx.T, w_idx_t, k_idx, k3, vt)


def _s5_kernel(x_ref, r0_ref, per_ref, pei_ref, qre_ref, qim_ref, lam_ref, d_ref, y_ref,
               mt_ref, er_ref, ei_ref, sr_ref, si_ref):
    width = SSM_T * SSM_GROUP_CH
    r0 = r0_ref[0]
    lane = lax.broadcasted_iota(jnp.int32, r0.shape, 1)
    for s in range(SSM_T):
        shifted = r0 if s == 0 else pltpu.roll(r0, s * SSM_GROUP_CH, 1)
        mt_ref[s * SSM_GROUP_CH:(s + 1) * SSM_GROUP_CH, :] = jnp.where(
            lane >= s * SSM_GROUP_CH, shifted, 0.0).astype(BF16)
    x = x_ref[0]
    xb = x.astype(BF16)
    er_ref[...] = jnp.dot(xb, per_ref[0], preferred_element_type=F32)
    ei_ref[...] = jnp.dot(xb, pei_ref[0], preferred_element_type=F32)
    lam_r = lam_ref[0, 0:1, :]
    lam_i = lam_ref[0, 1:2, :]
    n_chunks = x.shape[0]

    def scan_body(c, carry):
        s_r, s_i = carry
        sr_ref[pl.ds(c, 1), :] = s_r
        si_ref[pl.ds(c, 1), :] = s_i
        e_r = er_ref[pl.ds(c, 1), :]
        e_i = ei_ref[pl.ds(c, 1), :]
        return (lam_r * s_r - lam_i * s_i + e_r, lam_r * s_i + lam_i * s_r + e_i)

    zero = jnp.zeros((1, SSM_STATE), F32)
    lax.fori_loop(0, n_chunks, scan_body, (zero, zero))
    y = jnp.dot(xb, mt_ref[...], preferred_element_type=F32)
    y = y + jnp.dot(sr_ref[...].astype(BF16), qre_ref[0], preferred_element_type=F32)
    y = y + jnp.dot(si_ref[...].astype(BF16), qim_ref[0], preferred_element_type=F32)
    y_ref[0] = y + d_ref[0] * x
    del width


def _s5(u, a_re, a_im, log_dt, b_re, b_im, c_re, c_im, d_skip):
    l = u.shape[0]
    t, c, p, g = SSM_T, SSM_GROUP_CH, SSM_STATE, SSM_GROUPS
    n_chunks = l // t
    width = t * c
    dt = jnp.exp(log_dt.astype(F32))[:, None]
    ar, ai = a_re.astype(F32), a_im.astype(F32)
    mag = jnp.exp(ar * dt)
    lb_re, lb_im = mag * jnp.cos(ai * dt), mag * jnp.sin(ai * dt)
    den = ar * ar + ai * ai
    coef_re = ((lb_re - 1.0) * ar + lb_im * ai) / den
    coef_im = (lb_im * ar - (lb_re - 1.0) * ai) / den
    br, bi = b_re.astype(F32), b_im.astype(F32)
    bb_re = coef_re[..., None] * br - coef_im[..., None] * bi
    bb_im = coef_re[..., None] * bi + coef_im[..., None] * br
    cr, ci = c_re.astype(F32), c_im.astype(F32)
    pr, pi = jnp.ones((g, 1, p), F32), jnp.zeros((g, 1, p), F32)
    sr, si = lb_re[:, None, :], lb_im[:, None, :]
    while pr.shape[1] < t + 1:
        pr, pi = (jnp.concatenate([pr, pr * sr - pi * si], axis=1),
                  jnp.concatenate([pi, pr * si + pi * sr], axis=1))
        sr, si = sr * sr - si * si, 2.0 * sr * si
    pr, pi = pr[:, :t + 1], pi[:, :t + 1]
    gc_re = cr[:, :, None, :] * pr[:, None] - ci[:, :, None, :] * pi[:, None]
    gc_im = cr[:, :, None, :] * pi[:, None] + ci[:, :, None, :] * pr[:, None]
    hp = lax.Precision.HIGHEST
    kj = (jnp.einsum("gojp,gpi->gijo", gc_re[:, :, :t], bb_re, precision=hp)
          - jnp.einsum("gojp,gpi->gijo", gc_im[:, :, :t], bb_im, precision=hp))
    r0 = kj.reshape(g, c, width)
    rr, ri = pr[:, :t][:, ::-1][:, :, None, :], pi[:, :t][:, ::-1][:, :, None, :]
    tb_re, tb_im = bb_re.transpose(0, 2, 1)[:, None], bb_im.transpose(0, 2, 1)[:, None]
    pe_re = (rr * tb_re - ri * tb_im).reshape(g, width, p)
    pe_im = (rr * tb_im + ri * tb_re).reshape(g, width, p)
    q_re = gc_re[:, :, 1:t + 1].transpose(0, 3, 2, 1).reshape(g, p, width)
    q_im = gc_im[:, :, 1:t + 1].transpose(0, 3, 2, 1).reshape(g, p, width)
    lam_pack = jnp.stack([pr[:, t], pi[:, t]], axis=1)
    d_tiled = jnp.tile(d_skip.astype(F32), (1, t)).reshape(g, 1, width)
    x = u.reshape(n_chunks, t, g, c).transpose(2, 0, 1, 3).reshape(g, n_chunks, width)
    grp3 = lambda a, b: pl.BlockSpec((1, a, b), lambda i: (i, 0, 0))
    y = pl.pallas_call(
        _s5_kernel,
        grid=(g,),
        in_specs=[grp3(n_chunks, width), grp3(c, width), grp3(width, p), grp3(width, p),
                  grp3(p, width), grp3(p, width), grp3(2, p), grp3(1, width)],
        out_specs=grp3(n_chunks, width),
        out_shape=jax.ShapeDtypeStruct((g, n_chunks, width), F32),
        scratch_shapes=[
            pltpu.VMEM((width, width), BF16),
            pltpu.VMEM((n_chunks, p), F32),
            pltpu.VMEM((n_chunks, p), F32),
            pltpu.VMEM((n_chunks, p), F32),
            pltpu.VMEM((n_chunks, p), F32),
        ],
        compiler_params=_params(("parallel",)),
        name="s5",
    )(x, r0, pe_re.astype(BF16), pe_im.astype(BF16), q_re.astype(BF16), (-q_im).astype(BF16),
      lam_pack, d_tiled)
    return y.reshape(g, n_chunks, t, c).transpose(1, 2, 0, 3).reshape(l, SSM_WIDTH)


def _gelu_tanh(x):
    return 0.5 * x * (1.0 + jnp.tanh(math.sqrt(2.0 / math.pi) * (x + 0.044715 * (x * x * x))))


def _out_kernel(att_ref, ag_ref, ys_ref, sg_ref, gate_ref, x_ref, watt_ref, wglu_ref, bglu_ref, wssm_ref,
                wout_ref, gfin_ref, o_ref):
    a = (att_ref[...].astype(F32) * ag_ref[...].astype(F32)).astype(BF16)
    y_a = jnp.dot(a, watt_ref[...], preferred_element_type=F32)
    s = _gelu_tanh(ys_ref[...])
    z = jnp.dot(s.astype(BF16), wglu_ref[...], preferred_element_type=F32) + bglu_ref[...]
    s = s * jax.nn.sigmoid(z)
    y_b = jnp.dot((s * sg_ref[...]).astype(BF16), wssm_ref[...], preferred_element_type=F32)
    gates = gate_ref[...].astype(F32)
    mixed = gates[:, :D_MODEL] * y_a + gates[:, D_MODEL:] * y_b
    r = x_ref[...] + jnp.dot(mixed.astype(BF16), wout_ref[...], preferred_element_type=F32)
    y = r * lax.rsqrt(jnp.mean(r * r, axis=-1, keepdims=True) + EPS)
    o_ref[...] = (y * gfin_ref[...]).astype(o_ref.dtype)


def _out_stage(att, att_gate, y_ssm, ssm_gate, gates, x, w_att_out, w_glu, b_glu, w_ssm_out, w_out, g_final, tm=256):
    l = x.shape[0]
    row = lambda n: pl.BlockSpec((tm, n), lambda i: (i, 0))
    return pl.pallas_call(
        _out_kernel,
        grid=(l // tm,),
        in_specs=[row(ATT_WIDTH), row(ATT_WIDTH), row(SSM_WIDTH), row(SSM_WIDTH), row(N_BRANCH * D_MODEL),
                  row(D_MODEL),
                  _resident((ATT_WIDTH, D_MODEL)), _resident((SSM_WIDTH, SSM_WIDTH)), _resident((1, SSM_WIDTH)),
                  _resident((SSM_WIDTH, D_MODEL)), _resident((D_MODEL, D_MODEL)), _resident((1, D_MODEL))],
        out_specs=row(D_MODEL),
        out_shape=jax.ShapeDtypeStruct((l, D_MODEL), x.dtype),
        compiler_params=_params(("parallel",)),
        name="out_stage",
    )(att, att_gate, y_ssm, ssm_gate, gates, x, w_att_out.astype(BF16), w_glu.astype(BF16),
      b_glu.reshape(1, -1), w_ssm_out.astype(BF16), w_out.astype(BF16), g_final.reshape(1, -1))


def _layer(x, positions, g_norm, w_in, b_merge, a_re, a_im, log_dt, ssm_b_re, ssm_b_im, ssm_c_re, ssm_c_im,
           ssm_d, w_glu, b_glu, w_att_out, w_ssm_out, w_out, g_out):
    off = [0]
    for s in SPLITS:
        off.append(off[-1] + s)
    wq, wk, wv, wag, wqi, wki, wwi, wu, wsg, wmg = (w_in[:, off[n]:off[n + 1]] for n in range(10))
    pad = jnp.zeros((D_MODEL, LANES - IDX_DIM - IDX_HEADS), w_in.dtype)
    groups = (("q", wq), ("att_gate", wag), ("q_idx", wqi), ("merge", wmg), ("kv", wk), ("v", wv), ("u", wu),
              ("ssm_gate", wsg), ("idx_misc", wki), ("w_idx", wwi), ("pad", pad))
    w_all = jnp.concatenate([w for _, w in groups], axis=1).astype(BF16)
    col, c0 = {}, 0
    for name, w in groups:
        col[name] = c0
        c0 += w.shape[1]
    rope = _rope_tables(positions)
    h = _rmsnorm(x, g_norm, BF16)

    q_scale = HEAD_DIM ** -0.5 * LOG2E
    q = _project("q", h, w_all, col["q"], ATT_WIDTH, ((0, ATT_WIDTH, "rope128", q_scale),), BF16, rope)
    kv = _project("kv", h, w_all, col["kv"], 2 * KV_WIDTH,
                  ((0, KV_WIDTH, "rope128", 1.0), (KV_WIDTH, KV_WIDTH, "none", 1.0)), BF16, rope)
    att_gate = _project("att_gate", h, w_all, col["att_gate"], ATT_WIDTH, ((0, ATT_WIDTH, "silu", 1.0),), BF16)
    q_idx = _project("q_idx", h, w_all, col["q_idx"], IDX_HEADS * IDX_DIM,
                     ((0, IDX_HEADS * IDX_DIM, "rope64", 1.0),), BF16, rope)
    misc = _project("idx_misc", h, w_all, col["idx_misc"], LANES,
                    ((0, LANES, "idx_misc", IDX_HEADS ** -0.5 * IDX_DIM ** -0.5),), F32, rope)
    u = _project("u", h, w_all, col["u"], SSM_WIDTH, ((0, SSM_WIDTH, "none", 1.0),), F32)
    ssm_gate = _project("ssm_gate", h, w_all, col["ssm_gate"], SSM_WIDTH, ((0, SSM_WIDTH, "silu", 1.0),), BF16)
    gates = _project("merge", h, w_all, col["merge"], N_BRANCH * D_MODEL, ((0, 1024, "sigmoid", 1.0),), BF16,
                     bias=b_merge, tn=1024)

    k_idx = misc[:, :IDX_DIM].astype(BF16)
    w_idx_t = misc[:, IDX_DIM:IDX_DIM + IDX_HEADS].T
    att = _dsa(q, kv[:, :KV_WIDTH], kv[:, KV_WIDTH:], q_idx, k_idx, w_idx_t, BF16)
    y_ssm = _s5(u, a_re, a_im, log_dt, ssm_b_re, ssm_b_im, ssm_c_re, ssm_c_im, ssm_d)
    return _out_stage(att, att_gate, y_ssm, ssm_gate, gates, x, w_att_out, w_glu, b_glu, w_ssm_out, w_out, g_out)


def kernel(x, positions, g_norm, w_in, b_merge, a_re, a_im, log_dt, ssm_b_re, ssm_b_im, ssm_c_re, ssm_c_im,
           ssm_d, w_glu, b_glu, w_att_out, w_ssm_out, w_out, g_final):
    depth = w_in.shape[0]
    assert depth == 1 and x.shape[0] == 1, "single layer, single sequence"
    y = _layer(x[0], positions[0], g_norm[0], w_in[0], b_merge[0], a_re[0], a_im[0], log_dt[0], ssm_b_re[0],
               ssm_b_im[0], ssm_c_re[0], ssm_c_im[0], ssm_d[0], w_glu[0], b_glu[0], w_att_out[0], w_ssm_out[0],
               w_out[0], g_final)
    return y[None]
```

```python
import functools
import math

import jax
import jax.numpy as jnp
from jax import lax
from jax.experimental import pallas as pl
from jax.experimental.pallas import tpu as pltpu

D_MODEL = 2048
CHUNK = 64
EPS = 1e-6
ATT_HEADS = 8
ATT_KV_HEADS = 2
HEAD_DIM = 128
ATT_WIDTH = ATT_HEADS * HEAD_DIM
KV_WIDTH = ATT_KV_HEADS * HEAD_DIM
IDX_HEADS = 16
IDX_DIM = 64
INDEX_TOPK = 256
ROPE_THETA = 500000.0
ROPE_FRACTION = 4
SSM_GROUP_CH = 16
SSM_GROUPS = 32
SSM_WIDTH = SSM_GROUPS * SSM_GROUP_CH
SSM_STATE = 64
N_BRANCH = 2
SPLITS = (ATT_WIDTH, KV_WIDTH, KV_WIDTH, ATT_WIDTH, IDX_HEADS * IDX_DIM, IDX_DIM, IDX_HEADS,
          SSM_WIDTH, SSM_WIDTH, N_BRANCH * D_MODEL)

LANES = 128
ROPE_FREQS = HEAD_DIM // ROPE_FRACTION // 2
VMEM_LIMIT = 56 * 1024 * 1024

Q_TILE = 256
KEY_TILE = 256
ATT_CHUNK = 512
SSM_T = 64
LOG2E = 1.4426950408889634
NEG_BIG = -1e30

F32 = jnp.float32
BF16 = jnp.bfloat16


def _params(semantics):
    return pltpu.CompilerParams(dimension_semantics=semantics, vmem_limit_bytes=VMEM_LIMIT)


def _resident(shape):
    zeros = (0,) * len(shape)
    return pl.BlockSpec(shape, lambda *_: zeros, pipeline_mode=pl.Buffered(1))


def _rope_kernel(pos_ref, freq_ref, cos_ref, sin_ref):
    ang = freq_ref[...] * pos_ref[...].astype(F32)
    cos_ref[...] = jnp.cos(ang)
    sin_ref[...] = jnp.sin(ang)


def _rope_tables(positions, head_dim):
    l = positions.shape[0]
    rd = head_dim // ROPE_FRACTION
    half = rd // 2
    inv_freq = jnp.exp(-math.log(ROPE_THETA) * jnp.arange(half, dtype=F32) * (2.0 / rd))
    return pl.pallas_call(
        _rope_kernel,
        out_shape=(jax.ShapeDtypeStruct((half, l), F32),) * 2,
        name="rope_tables_%d" % head_dim,
    )(positions.reshape(1, l), inv_freq.reshape(half, 1))


def _rmsnorm_kernel(x_ref, g_ref, o_ref):
    x = x_ref[...]
    y = x * lax.rsqrt(jnp.mean(x * x, axis=-1, keepdims=True) + EPS)
    o_ref[...] = (y * g_ref[...]).astype(o_ref.dtype)


def _rmsnorm(x, g, out_dtype, tm=512):
    l, d = x.shape
    return pl.pallas_call(
        _rmsnorm_kernel,
        grid=(l // tm,),
        in_specs=[pl.BlockSpec((tm, d), lambda i: (i, 0)), pl.BlockSpec((1, d), lambda i: (0, 0))],
        out_specs=pl.BlockSpec((tm, d), lambda i: (i, 0)),
        out_shape=jax.ShapeDtypeStruct((l, d), out_dtype),
        compiler_params=_params(("parallel",)),
        name="rmsnorm",
    )(x, g.reshape(1, d))


def _silu(x):
    return x * jax.nn.sigmoid(x)


def _rotate_rows(xt, cos, sin, head_dim):
    half = head_dim // ROPE_FRACTION // 2
    pieces = []
    for base in range(0, LANES, head_dim):
        top, bot = xt[base:base + half], xt[base + half:base + 2 * half]
        pieces += [top * cos - bot * sin, bot * cos + top * sin, xt[base + 2 * half:base + head_dim]]
    return jnp.concatenate(pieces, axis=0)


def _proj_kernel(*refs, segments, has_rope, has_bias, transpose_out):
    h_ref, w_ref = refs[0], refs[1]
    pos = 2
    if has_bias:
        bias_ref = refs[pos]
        pos += 1
    if has_rope:
        cos, sin = refs[pos][...], refs[pos + 1][...]
        pos += 2
    o_ref = refs[pos]
    acc = jnp.dot(h_ref[...], w_ref[...], preferred_element_type=F32)
    if has_bias:
        acc = acc + bias_ref[...]
    for start, size, kind, scale in segments:
        for b in range(size // LANES):
            lo = start + b * LANES
            xb = acc[:, lo:lo + LANES]
            if kind == "silu":
                xb = _silu(xb)
            elif kind == "sigmoid":
                xb = jax.nn.sigmoid(xb)
            if not transpose_out:
                assert kind in ("none", "silu", "sigmoid") and scale == 1.0
                o_ref[:, lo:lo + LANES] = xb.astype(o_ref.dtype)
                continue
            xt = xb.T
            if kind == "rope128":
                xt = _rotate_rows(xt, cos, sin, HEAD_DIM)
            elif kind == "rope64":
                xt = _rotate_rows(xt, cos, sin, IDX_DIM)
            elif kind == "idx_misc":
                xt = jnp.concatenate([_rotate_rows(xt, cos, sin, IDX_DIM)[:IDX_DIM], xt[IDX_DIM:] * scale], axis=0)
            if kind != "idx_misc" and scale != 1.0:
                xt = xt * scale
            o_ref[lo:lo + LANES, :] = xt.astype(o_ref.dtype)


def _project(name, h, w, col0, n, segments, out_dtype, rope=None, bias=None, tm=512, tn=None,
             transpose_out=False):
    l, d = h.shape
    tn = n if tn is None else tn
    assert col0 % tn == 0 and n % tn == 0
    jb = col0 // tn
    if transpose_out:
        out_spec = pl.BlockSpec((tn, tm), lambda i, j: (j, i))
        out_shape = jax.ShapeDtypeStruct((n, l), out_dtype)
    else:
        out_spec = pl.BlockSpec((tm, tn), lambda i, j: (i, j))
        out_shape = jax.ShapeDtypeStruct((l, n), out_dtype)
    in_specs = [pl.BlockSpec((tm, d), lambda i, j: (i, 0)), pl.BlockSpec((d, tn), lambda i, j: (0, jb + j))]
    args = [h, w]
    if bias is not None:
        in_specs.append(pl.BlockSpec((1, tn), lambda i, j: (0, j)))
        args.append(bias.reshape(1, n))
    if rope is not None:
        assert transpose_out
        in_specs += [pl.BlockSpec((rope[0].shape[0], tm), lambda i, j: (0, i))] * 2
        args += list(rope)
    kern = functools.partial(_proj_kernel, segments=segments, has_rope=rope is not None, has_bias=bias is not None,
                             transpose_out=transpose_out)
    return pl.pallas_call(
        kern,
        grid=(l // tm, n // tn),
        in_specs=in_specs,
        out_specs=out_spec,
        out_shape=out_shape,
        compiler_params=_params(("parallel", "arbitrary")),
        name="in_proj_" + name,
    )(*args)


def _col_reduce(x, op):
    r, q = x.shape
    y = op(x.reshape(r // 64, 64, q), axis=0) if r > 64 else x
    y = op(y.reshape(8, 8, q), axis=0)
    return op(y, axis=0, keepdims=True)


def _dsa_kernel(qt_ref, qit_ref, wt_ref, kidx_ref, k_ref, vt_ref, o_ref,
                sc_ref, s_ref, acc_ref, m_ref, l_ref, mx_ref, *, topk, n_att_chunks):
    i = pl.program_id(0)
    n_blk = i + 1
    sub = 8
    grp = ATT_HEADS // ATT_KV_HEADS
    inf = float("inf")

    def score_block(t, carry, diagonal):
        smin, smax = carry
        rows = pl.ds(pl.multiple_of(t * KEY_TILE, KEY_TILE), KEY_TILE)
        kt = kidx_ref[rows, :]
        acc = jnp.zeros((KEY_TILE, Q_TILE), F32)
        for h in range(IDX_HEADS):
            logit = jnp.dot(kt, qit_ref[h * IDX_DIM:(h + 1) * IDX_DIM, :], preferred_element_type=F32)
            acc = acc + wt_ref[h:h + 1, :] * jnp.maximum(logit, 0.0)
        lo_src = acc
        if diagonal:
            r_chunk = lax.broadcasted_iota(jnp.int32, acc.shape, 0) // CHUNK
            q_chunk = lax.broadcasted_iota(jnp.int32, acc.shape, 1) // CHUNK
            lo_src = jnp.where(r_chunk <= q_chunk, acc, inf)
            acc = jnp.where(r_chunk <= q_chunk, acc, -inf)
        sc_ref[rows, :] = acc
        return (jnp.minimum(smin, jnp.min(lo_src.reshape(KEY_TILE // sub, sub, Q_TILE), axis=0)),
                jnp.maximum(smax, jnp.max(acc.reshape(KEY_TILE // sub, sub, Q_TILE), axis=0)))

    carry = lax.fori_loop(0, i, lambda t, c: score_block(t, c, False),
                          (jnp.full((sub, Q_TILE), inf, F32), jnp.full((sub, Q_TILE), -inf, F32)))
    smin, smax = score_block(i, carry, True)
    smin = jnp.min(smin, axis=0, keepdims=True)
    smax = jnp.max(smax, axis=0, keepdims=True)

    blk_per_it = 2 * ATT_CHUNK // KEY_TILE
    n_it = (n_blk + blk_per_it - 1) // blk_per_it

    def fill_body(t, carry):
        sc_ref[pl.ds(pl.multiple_of(t * KEY_TILE, KEY_TILE), KEY_TILE), :] = jnp.full(
            (KEY_TILE, Q_TILE), -inf, F32)
        return carry

    lax.fori_loop(n_blk, n_it * blk_per_it, fill_body, 0)

    def count_ge(cand):
        def count_body(t, accs):
            blk = sc_ref[pl.ds(pl.multiple_of(t * KEY_TILE, KEY_TILE), KEY_TILE), :]
            hit = (blk >= cand).astype(jnp.int32)
            return accs + jnp.sum(hit.reshape(4, KEY_TILE // (4 * sub), sub, Q_TILE), axis=1)

        accs = lax.fori_loop(0, n_blk, count_body, jnp.zeros((4, sub, Q_TILE), jnp.int32))
        return jnp.sum(jnp.sum(accs, axis=0), axis=0, keepdims=True)

    lane_chunk = lax.broadcasted_iota(jnp.int32, (1, Q_TILE), 1) // CHUNK
    n_adm = (i * (Q_TILE // CHUNK) + lane_chunk + 1) * CHUNK
    hi0 = smax + 0.5 * (smax - smin) + 1e-30

    def bis_body(carry):
        _, lo, hi, cnt_lo, live_i = carry
        mid = lo + 0.5 * (hi - lo)
        live = (live_i > 0) & (mid > lo) & (mid < hi)
        cnt = count_ge(mid)
        ge = cnt >= topk
        lo = jnp.where(live & ge, mid, lo)
        hi = jnp.where(live & ~ge, mid, hi)
        cnt_lo = jnp.where(live & ge, cnt, cnt_lo)
        live_i = (live & (cnt_lo > topk)).astype(jnp.int32)
        return jnp.max(live_i), lo, hi, cnt_lo, live_i

    def bis_cond(carry):
        return carry[0] > 0

    live0 = (n_adm > topk).astype(jnp.int32)
    _, thr, _, _, _ = lax.while_loop(bis_cond, bis_body, (jnp.max(live0), smin, hi0, n_adm, live0))

    last_chunk = n_att_chunks - 1

    def qk(h, c):
        rows = pl.ds(pl.multiple_of(c * ATT_CHUNK, ATT_CHUNK), ATT_CHUNK)
        return jnp.dot(k_ref[h // grp, rows, :], qt_ref[h * HEAD_DIM:(h + 1) * HEAD_DIM, :],
                       preferred_element_type=F32)

    def masked_scores(heads, c, slot):
        rows = pl.ds(pl.multiple_of(c * ATT_CHUNK, ATT_CHUNK), ATT_CHUNK)
        bias = jnp.where(sc_ref[rows, :] >= thr, 0.0, NEG_BIG)
        maxima = []
        for g, h in enumerate(heads):
            s = qk(h, c) + bias
            s_ref[slot, g] = s
            maxima.append(_col_reduce(s, jnp.max))
        return maxima

    for kv in range(ATT_KV_HEADS):
        heads = range(kv * grp, (kv + 1) * grp)
        acc_ref[...] = jnp.zeros(acc_ref.shape, F32)
        m_ref[...] = jnp.full(m_ref.shape, NEG_BIG, F32)
        l_ref[...] = jnp.zeros(l_ref.shape, F32)
        for g, mx in enumerate(masked_scores(heads, 0, 0)):
            mx_ref[g:g + 1, :] = mx

        def att_body(it, carry, kv=kv, heads=heads):
            for si in range(2):
                c = it * 2 + si
                v_t = vt_ref[c, kv * HEAD_DIM:(kv + 1) * HEAD_DIM, :]
                mx_next = masked_scores(heads, jnp.minimum(c + 1, last_chunk), 1 - si)
                for g, h in enumerate(heads):
                    m_old = m_ref[g:g + 1, :]
                    m_new = jnp.maximum(m_old, mx_ref[g:g + 1, :])
                    alpha = jnp.exp2(m_old - m_new)
                    p = jnp.exp2(s_ref[si, g] - m_new)
                    l_ref[g:g + 1, :] = alpha * l_ref[g:g + 1, :] + _col_reduce(p, jnp.sum)
                    pv = jnp.dot(v_t, p.astype(BF16), preferred_element_type=F32)
                    acc_ref[g] = alpha * acc_ref[g] + pv
                    m_ref[g:g + 1, :] = m_new
                for g, mx in enumerate(mx_next):
                    mx_ref[g:g + 1, :] = mx
            return carry

        lax.fori_loop(0, n_it, att_body, 0)
        for g, h in enumerate(heads):
            o_ref[:, h * HEAD_DIM:(h + 1) * HEAD_DIM] = (acc_ref[g] / l_ref[g:g + 1, :]).T.astype(o_ref.dtype)


def _dsa(q_t, k_t, v_t, q_idx_t, k_idx, w_idx_t, out_dtype):
    l = q_t.shape[1]
    topk = min(INDEX_TOPK, l // 4)
    assert l % (2 * ATT_CHUNK) == 0
    n_att_chunks = l // ATT_CHUNK
    k3 = k_t.reshape(ATT_KV_HEADS, HEAD_DIM, l).transpose(0, 2, 1)
    vt = v_t.reshape(KV_WIDTH, n_att_chunks, ATT_CHUNK).transpose(1, 0, 2)
    kern = functools.partial(_dsa_kernel, topk=topk, n_att_chunks=n_att_chunks)
    return pl.pallas_call(
        kern,
        grid=(l // Q_TILE,),
        in_specs=[
            pl.BlockSpec((ATT_WIDTH, Q_TILE), lambda i: (0, i)),
            pl.BlockSpec((IDX_HEADS * IDX_DIM, Q_TILE), lambda i: (0, i)),
            pl.BlockSpec((IDX_HEADS, Q_TILE), lambda i: (0, i)),
            _resident((l, IDX_DIM)),
            _resident((ATT_KV_HEADS, l, HEAD_DIM)),
            _resident((n_att_chunks, KV_WIDTH, ATT_CHUNK)),
        ],
        out_specs=pl.BlockSpec((Q_TILE, ATT_WIDTH), lambda i: (i, 0)),
        out_shape=jax.ShapeDtypeStruct((l, ATT_WIDTH), out_dtype),
        scratch_shapes=[
            pltpu.VMEM((l, Q_TILE), F32),
            pltpu.VMEM((2, ATT_HEADS // ATT_KV_HEADS, ATT_CHUNK, Q_TILE), F32),
            pltpu.VMEM((ATT_HEADS // ATT_KV_HEADS, HEAD_DIM, Q_TILE), F32),
        ] + [pltpu.VMEM((ATT_HEADS // ATT_KV_HEADS, Q_TILE), F32)] * 3,
        compiler_params=_params(("arbitrary",)),
        name="dsa",
    )(q_t, q_idx_t, w_idx_t, k_idx, k3, vt)


def _s5_kernel(x_ref, r0_ref, per_ref, pei_ref, qre_ref, qim_ref, lam_ref, y_ref,
               mt_ref, er_ref, ei_ref, sr_ref, si_ref):
    r0 = r0_ref[0]
    lane = lax.broadcasted_iota(jnp.int32, r0.shape, 1)
    for s in range(SSM_T):
        shifted = r0 if s == 0 else pltpu.roll(r0, s * SSM_GROUP_CH, 1)
        mt_ref[s * SSM_GROUP_CH:(s + 1) * SSM_GROUP_CH, :] = jnp.where(
            lane >= s * SSM_GROUP_CH, shifted, 0.0).astype(BF16)
    xb = x_ref[0]
    er_ref[...] = jnp.dot(xb, per_ref[0], preferred_element_type=F32)
    ei_ref[...] = jnp.dot(xb, pei_ref[0], preferred_element_type=F32)
    lam_r = lam_ref[0, 0:1, :]
    lam_i = lam_ref[0, 1:2, :]
    n_chunks = xb.shape[0]

    def scan_body(c, carry):
        s_r, s_i = carry
        sr_ref[pl.ds(c, 1), :] = s_r
        si_ref[pl.ds(c, 1), :] = s_i
        e_r = er_ref[pl.ds(c, 1), :]
        e_i = ei_ref[pl.ds(c, 1), :]
        return (lam_r * s_r - lam_i * s_i + e_r, lam_r * s_i + lam_i * s_r + e_i)

    zero = jnp.zeros((1, SSM_STATE), F32)
    lax.fori_loop(0, n_chunks, scan_body, (zero, zero))
    y = jnp.dot(xb, mt_ref[...], preferred_element_type=F32)
    y = y + jnp.dot(sr_ref[...].astype(BF16), qre_ref[0], preferred_element_type=F32)
    y = y + jnp.dot(si_ref[...].astype(BF16), qim_ref[0], preferred_element_type=F32)
    y_ref[0] = y.astype(y_ref.dtype)


def _s5(u, a_re, a_im, log_dt, b_re, b_im, c_re, c_im):
    l = u.shape[0]
    t, c, p, g = SSM_T, SSM_GROUP_CH, SSM_STATE, SSM_GROUPS
    n_chunks = l // t
    width = t * c
    dt = jnp.exp(log_dt.astype(F32))[:, None]
    ar, ai = a_re.astype(F32), a_im.astype(F32)
    mag = jnp.exp(ar * dt)
    lb_re, lb_im = mag * jnp.cos(ai * dt), mag * jnp.sin(ai * dt)
    den = ar * ar + ai * ai
    coef_re = ((lb_re - 1.0) * ar + lb_im * ai) / den
    coef_im = (lb_im * ar - (lb_re - 1.0) * ai) / den
    br, bi = b_re.astype(F32), b_im.astype(F32)
    bb_re = coef_re[..., None] * br - coef_im[..., None] * bi
    bb_im = coef_re[..., None] * bi + coef_im[..., None] * br
    cr, ci = c_re.astype(F32), c_im.astype(F32)
    pr, pi = jnp.ones((g, 1, p), F32), jnp.zeros((g, 1, p), F32)
    sr, si = lb_re[:, None, :], lb_im[:, None, :]
    while pr.shape[1] < t + 1:
        pr, pi = (jnp.concatenate([pr, pr * sr - pi * si], axis=1),
                  jnp.concatenate([pi, pr * si + pi * sr], axis=1))
        sr, si = sr * sr - si * si, 2.0 * sr * si
    pr, pi = pr[:, :t + 1], pi[:, :t + 1]
    gc_re = cr[:, :, None, :] * pr[:, None] - ci[:, :, None, :] * pi[:, None]
    gc_im = cr[:, :, None, :] * pi[:, None] + ci[:, :, None, :] * pr[:, None]
    hp = lax.Precision.HIGHEST
    kj = (jnp.einsum("gojp,gpi->gijo", gc_re[:, :, :t], bb_re, precision=hp)
          - jnp.einsum("gojp,gpi->gijo", gc_im[:, :, :t], bb_im, precision=hp))
    r0 = kj.reshape(g, c, width)
    rr, ri = pr[:, :t][:, ::-1][:, :, None, :], pi[:, :t][:, ::-1][:, :, None, :]
    tb_re, tb_im = bb_re.transpose(0, 2, 1)[:, None], bb_im.transpose(0, 2, 1)[:, None]
    pe_re = (rr * tb_re - ri * tb_im).reshape(g, width, p)
    pe_im = (rr * tb_im + ri * tb_re).reshape(g, width, p)
    q_re = gc_re[:, :, 1:t + 1].transpose(0, 3, 2, 1).reshape(g, p, width)
    q_im = gc_im[:, :, 1:t + 1].transpose(0, 3, 2, 1).reshape(g, p, width)
    lam_pack = jnp.stack([pr[:, t], pi[:, t]], axis=1)
    x = u.astype(BF16).reshape(n_chunks, t, g, c).transpose(2, 0, 1, 3).reshape(g, n_chunks, width)
    grp3 = lambda a, b: pl.BlockSpec((1, a, b), lambda i: (i, 0, 0))
    y = pl.pallas_call(
        _s5_kernel,
        grid=(g,),
        in_specs=[grp3(n_chunks, width), grp3(c, width), grp3(width, p), grp3(width, p),
                  grp3(p, width), grp3(p, width), grp3(2, p)],
        out_specs=grp3(n_chunks, width),
        out_shape=jax.ShapeDtypeStruct((g, n_chunks, width), BF16),
        scratch_shapes=[
            pltpu.VMEM((width, width), BF16),
            pltpu.VMEM((n_chunks, p), F32),
            pltpu.VMEM((n_chunks, p), F32),
            pltpu.VMEM((n_chunks, p), F32),
            pltpu.VMEM((n_chunks, p), F32),
        ],
        compiler_params=_params(("parallel",)),
        name="s5",
    )(x, r0, pe_re.astype(BF16), pe_im.astype(BF16), q_re.astype(BF16), (-q_im).astype(BF16), lam_pack)
    return y.reshape(g, n_chunks, t, c).transpose(1, 2, 0, 3).reshape(l, SSM_WIDTH)


def _gelu_tanh(x):
    return 0.5 * x * (1.0 + jnp.tanh(math.sqrt(2.0 / math.pi) * (x + 0.044715 * (x * x * x))))


def _out_kernel(att_ref, ag_ref, ys_ref, u_ref, sg_ref, gate_ref, x_ref, d_ref, watt_ref, wglu_ref, bglu_ref,
                wssm_ref, wout_ref, gfin_ref, o_ref):
    a = (att_ref[...].astype(F32) * ag_ref[...].astype(F32)).astype(BF16)
    y_a = jnp.dot(a, watt_ref[...], preferred_element_type=F32)
    s = _gelu_tanh(ys_ref[...].astype(F32) + d_ref[...] * u_ref[...])
    z = jnp.dot(s.astype(BF16), wglu_ref[...], preferred_element_type=F32) + bglu_ref[...]
    s = s * jax.nn.sigmoid(z)
    y_b = jnp.dot((s * sg_ref[...]).astype(BF16), wssm_ref[...], preferred_element_type=F32)
    gates = gate_ref[...].astype(F32)
    mixed = gates[:, :D_MODEL] * y_a + gates[:, D_MODEL:] * y_b
    r = x_ref[...] + jnp.dot(mixed.astype(BF16), wout_ref[...], preferred_element_type=F32)
    y = r * lax.rsqrt(jnp.mean(r * r, axis=-1, keepdims=True) + EPS)
    o_ref[...] = (y * gfin_ref[...]).astype(o_ref.dtype)


def _out_stage(att, att_gate, y_ssm, u, ssm_gate, gates, x, d_skip, w_att_out, w_glu, b_glu, w_ssm_out, w_out,
               g_final, tm=256):
    l = x.shape[0]
    row = lambda n: pl.BlockSpec((tm, n), lambda i: (i, 0))
    return pl.pallas_call(
        _out_kernel,
        grid=(l // tm,),
        in_specs=[row(ATT_WIDTH), row(ATT_WIDTH), row(SSM_WIDTH), row(SSM_WIDTH), row(SSM_WIDTH),
                  row(N_BRANCH * D_MODEL), row(D_MODEL), _resident((1, SSM_WIDTH)),
                  _resident((ATT_WIDTH, D_MODEL)), _resident((SSM_WIDTH, SSM_WIDTH)), _resident((1, SSM_WIDTH)),
                  _resident((SSM_WIDTH, D_MODEL)), _resident((D_MODEL, D_MODEL)), _resident((1, D_MODEL))],
        out_specs=row(D_MODEL),
        out_shape=jax.ShapeDtypeStruct((l, D_MODEL), x.dtype),
        compiler_params=_params(("parallel",)),
        name="out_stage",
    )(att, att_gate, y_ssm, u, ssm_gate, gates, x, d_skip.astype(F32).reshape(1, -1), w_att_out.astype(BF16),
      w_glu.astype(BF16), b_glu.reshape(1, -1), w_ssm_out.astype(BF16), w_out.astype(BF16), g_final.reshape(1, -1))


def _layer(x, positions, g_norm, w_in, b_merge, a_re, a_im, log_dt, ssm_b_re, ssm_b_im, ssm_c_re, ssm_c_im,
           ssm_d, w_glu, b_glu, w_att_out, w_ssm_out, w_out, g_out):
    off = [0]
    for s in SPLITS:
        off.append(off[-1] + s)
    head_end = off[7] + LANES - IDX_DIM - IDX_HEADS
    w_head = w_in[:, :head_end].astype(BF16)
    w_tail = w_in[:, off[7]:].astype(BF16)
    rope_att = _rope_tables(positions, HEAD_DIM)
    rope_idx = _rope_tables(positions, IDX_DIM)
    h = _rmsnorm(x, g_norm, BF16)

    q_scale = HEAD_DIM ** -0.5 * LOG2E
    half = ATT_WIDTH // 2
    q_t = _project("q", h, w_head, off[0], ATT_WIDTH, ((0, ATT_WIDTH, "rope128", q_scale),), BF16, rope_att,
                   transpose_out=True)
    kv_t = _project("kv", h, w_head, off[1], 2 * KV_WIDTH,
                    ((0, KV_WIDTH, "rope128", 1.0), (KV_WIDTH, KV_WIDTH, "none", 1.0)), BF16, rope_att,
                    transpose_out=True)
    att_gate = _project("att_gate", h, w_head, off[3], ATT_WIDTH, ((0, half, "silu", 1.0),), BF16, tn=half)
    q_idx_t = _project("q_idx", h, w_head, off[4], IDX_HEADS * IDX_DIM, ((0, half, "rope64", 1.0),), BF16,
                       rope_idx, tn=half, transpose_out=True)
    misc_t = _project("idx_misc", h, w_head, off[5], LANES,
                      ((0, LANES, "idx_misc", IDX_HEADS ** -0.5 * IDX_DIM ** -0.5),), F32, rope_idx,
                      transpose_out=True)
    u = _project("u", h, w_tail, 0, SSM_WIDTH, ((0, SSM_WIDTH, "none", 1.0),), F32)
    ssm_gate = _project("ssm_gate", h, w_tail, SSM_WIDTH, SSM_WIDTH, ((0, SSM_WIDTH, "silu", 1.0),), BF16)
    gates = _project("merge", h, w_tail, 2 * SSM_WIDTH, N_BRANCH * D_MODEL, ((0, 1024, "sigmoid", 1.0),), BF16,
                     bias=b_merge, tn=1024)

    k_idx = misc_t[:IDX_DIM].T.astype(BF16)
    w_idx_t = misc_t[IDX_DIM:IDX_DIM + IDX_HEADS]
    att = _dsa(q_t, kv_t[:KV_WIDTH], kv_t[KV_WIDTH:], q_idx_t, k_idx, w_idx_t, BF16)
    y_ssm = _s5(u, a_re, a_im, log_dt, ssm_b_re, ssm_b_im, ssm_c_re, ssm_c_im)
    return _out_stage(att, att_gate, y_ssm, u, ssm_gate, gates, x, ssm_d, w_att_out, w_glu, b_glu, w_ssm_out, w_out,
                      g_out)


def kernel(x, positions, g_norm, w_in, b_merge, a_re, a_im, log_dt, ssm_b_re, ssm_b_im, ssm_c_re, ssm_c_im,
           ssm_d, w_glu, b_glu, w_att_out, w_ssm_out, w_out, g_final):
    depth = w_in.shape[0]
    assert depth == 1 and x.shape[0] == 1, "single layer, single sequence"
    y = _layer(x[0], positions[0], g_norm[0], w_in[0], b_merge[0], a_re[0], a_im[0], log_dt[0], ssm_b_re[0],
               ssm_b_im[0], ssm_c_re[0], ssm_c_im[0], ssm_d[0], w_glu[0], b_glu[0], w_att_out[0], w_ssm_out[0],
               w_out[0], g_final)
    return y[None]
```

```python
import functools
import math

import jax
import jax.numpy as jnp
from jax import lax
from jax.experimental import pallas as pl
from jax.experimental.pallas import tpu as pltpu

D_MODEL = 2048
CHUNK = 64
EPS = 1e-6
ATT_HEADS = 8
ATT_KV_HEADS = 2
HEAD_DIM = 128
ATT_WIDTH = ATT_HEADS * HEAD_DIM
KV_WIDTH = ATT_KV_HEADS * HEAD_DIM
IDX_HEADS = 16
IDX_DIM = 64
INDEX_TOPK = 256
ROPE_THETA = 500000.0
ROPE_FRACTION = 4
SSM_GROUP_CH = 16
SSM_GROUPS = 32
SSM_WIDTH = SSM_GROUPS * SSM_GROUP_CH
SSM_STATE = 64
N_BRANCH = 2
SPLITS = (ATT_WIDTH, KV_WIDTH, KV_WIDTH, ATT_WIDTH, IDX_HEADS * IDX_DIM, IDX_DIM, IDX_HEADS,
          SSM_WIDTH, SSM_WIDTH, N_BRANCH * D_MODEL)

LANES = 128
ROPE_FREQS = HEAD_DIM // ROPE_FRACTION // 2
VMEM_LIMIT = 56 * 1024 * 1024

Q_TILE = 256
KEY_TILE = 256
ATT_CHUNK = 512
BLIND_PASSES = 14
SSM_T = 64
LOG2E = 1.4426950408889634
NEG_BIG = -1e30

F32 = jnp.float32
BF16 = jnp.bfloat16


def _params(semantics):
    return pltpu.CompilerParams(dimension_semantics=semantics, vmem_limit_bytes=VMEM_LIMIT)


def _resident(shape):
    zeros = (0,) * len(shape)
    return pl.BlockSpec(shape, lambda *_: zeros, pipeline_mode=pl.Buffered(1))


def _rope_kernel(pos_ref, freq_ref, cos_ref, sin_ref):
    ang = freq_ref[...] * pos_ref[...].astype(F32)
    cos_ref[...] = jnp.cos(ang)
    sin_ref[...] = jnp.sin(ang)


def _rope_tables(positions, head_dim):
    l = positions.shape[0]
    rd = head_dim // ROPE_FRACTION
    half = rd // 2
    inv_freq = jnp.exp(-math.log(ROPE_THETA) * jnp.arange(half, dtype=F32) * (2.0 / rd))
    return pl.pallas_call(
        _rope_kernel,
        out_shape=(jax.ShapeDtypeStruct((half, l), F32),) * 2,
        name="rope_tables_%d" % head_dim,
    )(positions.reshape(1, l), inv_freq.reshape(half, 1))


def _rmsnorm_kernel(x_ref, g_ref, o_ref):
    x = x_ref[...]
    y = x * lax.rsqrt(jnp.mean(x * x, axis=-1, keepdims=True) + EPS)
    o_ref[...] = (y * g_ref[...]).astype(o_ref.dtype)


def _rmsnorm(x, g, out_dtype, tm=512):
    l, d = x.shape
    return pl.pallas_call(
        _rmsnorm_kernel,
        grid=(l // tm,),
        in_specs=[pl.BlockSpec((tm, d), lambda i: (i, 0)), pl.BlockSpec((1, d), lambda i: (0, 0))],
        out_specs=pl.BlockSpec((tm, d), lambda i: (i, 0)),
        out_shape=jax.ShapeDtypeStruct((l, d), out_dtype),
        compiler_params=_params(("parallel",)),
        name="rmsnorm",
    )(x, g.reshape(1, d))


def _silu(x):
    return x * jax.nn.sigmoid(x)


def _rotate_rows(xt, cos, sin, head_dim):
    half = head_dim // ROPE_FRACTION // 2
    pieces = []
    for base in range(0, LANES, head_dim):
        top, bot = xt[base:base + half], xt[base + half:base + 2 * half]
        pieces += [top * cos - bot * sin, bot * cos + top * sin, xt[base + 2 * half:base + head_dim]]
    return jnp.concatenate(pieces, axis=0)


def _proj_kernel(*refs, segments, has_rope, has_bias, transpose_out):
    h_ref, w_ref = refs[0], refs[1]
    pos = 2
    if has_bias:
        bias_ref = refs[pos]
        pos += 1
    if has_rope:
        cos, sin = refs[pos][...], refs[pos + 1][...]
        pos += 2
    o_ref = refs[pos]
    acc = jnp.dot(h_ref[...], w_ref[...], preferred_element_type=F32)
    if has_bias:
        acc = acc + bias_ref[...]
    for start, size, kind, scale in segments:
        for b in range(size // LANES):
            lo = start + b * LANES
            xb = acc[:, lo:lo + LANES]
            if kind == "silu":
                xb = _silu(xb)
            elif kind == "sigmoid":
                xb = jax.nn.sigmoid(xb)
            if not transpose_out:
                assert kind in ("none", "silu", "sigmoid") and scale == 1.0
                o_ref[:, lo:lo + LANES] = xb.astype(o_ref.dtype)
                continue
            xt = xb.T
            if kind == "rope128":
                xt = _rotate_rows(xt, cos, sin, HEAD_DIM)
            elif kind == "rope64":
                xt = _rotate_rows(xt, cos, sin, IDX_DIM)
            elif kind == "idx_misc":
                xt = jnp.concatenate([_rotate_rows(xt, cos, sin, IDX_DIM)[:IDX_DIM], xt[IDX_DIM:] * scale], axis=0)
            if kind != "idx_misc" and scale != 1.0:
                xt = xt * scale
            o_ref[lo:lo + LANES, :] = xt.astype(o_ref.dtype)


def _project(name, h, w, col0, n, segments, out_dtype, rope=None, bias=None, tm=512, tn=None,
             transpose_out=False):
    l, d = h.shape
    tn = n if tn is None else tn
    assert col0 % tn == 0 and n % tn == 0
    jb = col0 // tn
    if transpose_out:
        out_spec = pl.BlockSpec((tn, tm), lambda i, j: (j, i))
        out_shape = jax.ShapeDtypeStruct((n, l), out_dtype)
    else:
        out_spec = pl.BlockSpec((tm, tn), lambda i, j: (i, j))
        out_shape = jax.ShapeDtypeStruct((l, n), out_dtype)
    in_specs = [pl.BlockSpec((tm, d), lambda i, j: (i, 0)), pl.BlockSpec((d, tn), lambda i, j: (0, jb + j))]
    args = [h, w]
    if bias is not None:
        in_specs.append(pl.BlockSpec((1, tn), lambda i, j: (0, j)))
        args.append(bias.reshape(1, n))
    if rope is not None:
        assert transpose_out
        in_specs += [pl.BlockSpec((rope[0].shape[0], tm), lambda i, j: (0, i))] * 2
        args += list(rope)
    kern = functools.partial(_proj_kernel, segments=segments, has_rope=rope is not None, has_bias=bias is not None,
                             transpose_out=transpose_out)
    return pl.pallas_call(
        kern,
        grid=(l // tm, n // tn),
        in_specs=in_specs,
        out_specs=out_spec,
        out_shape=out_shape,
        compiler_params=_params(("parallel", "arbitrary")),
        name="in_proj_" + name,
    )(*args)


def _col_reduce(x, op):
    r, q = x.shape
    y = op(x.reshape(r // 64, 64, q), axis=0) if r > 64 else x
    y = op(y.reshape(8, 8, q), axis=0)
    return op(y, axis=0, keepdims=True)


def _dsa_kernel(qt_ref, qit_ref, wt_ref, kidx_ref, k_ref, vt_ref, o_ref,
                sc_ref, s_ref, acc_ref, *, topk, n_att_chunks):
    i = pl.program_id(0)
    n_blk = i + 1
    sub = 8
    grp = ATT_HEADS // ATT_KV_HEADS
    inf = float("inf")

    def score_block(t, carry, diagonal):
        smin, smax = carry
        rows = pl.ds(pl.multiple_of(t * KEY_TILE, KEY_TILE), KEY_TILE)
        kt = kidx_ref[rows, :]
        acc = jnp.zeros((KEY_TILE, Q_TILE), F32)
        for h in range(IDX_HEADS):
            logit = jnp.dot(kt, qit_ref[h * IDX_DIM:(h + 1) * IDX_DIM, :], preferred_element_type=F32)
            acc = acc + wt_ref[h:h + 1, :] * jnp.maximum(logit, 0.0)
        lo_src = acc
        if diagonal:
            r_chunk = lax.broadcasted_iota(jnp.int32, acc.shape, 0) // CHUNK
            q_chunk = lax.broadcasted_iota(jnp.int32, acc.shape, 1) // CHUNK
            lo_src = jnp.where(r_chunk <= q_chunk, acc, inf)
            acc = jnp.where(r_chunk <= q_chunk, acc, -inf)
        sc_ref[rows, :] = acc
        return (jnp.minimum(smin, jnp.min(lo_src.reshape(KEY_TILE // sub, sub, Q_TILE), axis=0)),
                jnp.maximum(smax, jnp.max(acc.reshape(KEY_TILE // sub, sub, Q_TILE), axis=0)))

    carry = lax.fori_loop(0, i, lambda t, c: score_block(t, c, False),
                          (jnp.full((sub, Q_TILE), inf, F32), jnp.full((sub, Q_TILE), -inf, F32)))
    smin, smax = score_block(i, carry, True)
    smin = jnp.min(smin, axis=0, keepdims=True)
    smax = jnp.max(smax, axis=0, keepdims=True)

    blk_per_it = 2 * ATT_CHUNK // KEY_TILE
    n_it = (n_blk + blk_per_it - 1) // blk_per_it

    def fill_body(t, carry):
        sc_ref[pl.ds(pl.multiple_of(t * KEY_TILE, KEY_TILE), KEY_TILE), :] = jnp.full(
            (KEY_TILE, Q_TILE), -inf, F32)
        return carry

    lax.fori_loop(n_blk, n_it * blk_per_it, fill_body, 0)

    def count_ge(cand):
        def count_body(t, accs):
            blk = sc_ref[pl.ds(pl.multiple_of(t * KEY_TILE, KEY_TILE), KEY_TILE), :]
            hit = (blk >= cand).astype(jnp.int32)
            return accs + jnp.sum(hit.reshape(4, KEY_TILE // (4 * sub), sub, Q_TILE), axis=1)

        accs = lax.fori_loop(0, n_blk, count_body, jnp.zeros((4, sub, Q_TILE), jnp.int32))
        return jnp.sum(jnp.sum(accs, axis=0), axis=0, keepdims=True)

    lane_chunk = lax.broadcasted_iota(jnp.int32, (1, Q_TILE), 1) // CHUNK
    n_adm = (i * (Q_TILE // CHUNK) + lane_chunk + 1) * CHUNK
    hi0 = smax + 0.5 * (smax - smin) + (jnp.abs(smax) * 1e-6 + 1e-30)

    def bis_step(state):
        lo, hi, cnt_lo, cnt_hi, live_i = state
        mid = lo + 0.5 * (hi - lo)
        live = (live_i > 0) & (mid > lo) & (mid < hi)
        cnt = count_ge(mid)
        ge = cnt >= topk
        lo = jnp.where(live & ge, mid, lo)
        hi = jnp.where(live & ~ge, mid, hi)
        cnt_lo = jnp.where(live & ge, cnt, cnt_lo)
        cnt_hi = jnp.where(live & ~ge, cnt, cnt_hi)
        return lo, hi, cnt_lo, cnt_hi, (live & (cnt_lo > topk)).astype(jnp.int32)

    def bis_body(carry):
        state = bis_step(carry[1:])
        return (jnp.max(state[4]),) + state

    state = lax.fori_loop(0, BLIND_PASSES, lambda _, st: bis_step(st),
                          (smin, hi0, n_adm, jnp.zeros((1, Q_TILE), jnp.int32), (n_adm > topk).astype(jnp.int32)))
    _, thr, _, cnt_lo, cnt_hi, _ = lax.while_loop(lambda c: c[0] > 0, bis_body, (jnp.max(state[4]),) + state)

    tied = cnt_lo > topk

    @pl.when(jnp.max(tied.astype(jnp.int32)) > 0)
    def _():
        keep = (topk - cnt_hi).astype(F32)
        tri = (lax.broadcasted_iota(jnp.int32, (KEY_TILE, KEY_TILE), 0)
               >= lax.broadcasted_iota(jnp.int32, (KEY_TILE, KEY_TILE), 1)).astype(BF16)

        def fix_body(t, before):
            rows = pl.ds(pl.multiple_of(t * KEY_TILE, KEY_TILE), KEY_TILE)
            blk = sc_ref[rows, :]
            eq = (blk == thr) & tied
            eq_f = eq.astype(F32)
            rank = jnp.dot(tri, eq_f.astype(BF16), preferred_element_type=F32) - 1.0 + before
            sc_ref[rows, :] = jnp.where(eq & (rank >= keep), -inf, blk)
            return before + jnp.sum(eq_f, axis=0, keepdims=True)

        lax.fori_loop(0, n_blk, fix_body, jnp.zeros((1, Q_TILE), F32))

    last_chunk = n_att_chunks - 1

    def qk(h, c):
        rows = pl.ds(pl.multiple_of(c * ATT_CHUNK, ATT_CHUNK), ATT_CHUNK)
        return jnp.dot(k_ref[h // grp, rows, :], qt_ref[h * HEAD_DIM:(h + 1) * HEAD_DIM, :],
                       preferred_element_type=F32)

    for kv in range(ATT_KV_HEADS):
        heads = range(kv * grp, (kv + 1) * grp)
        acc_ref[...] = jnp.zeros(acc_ref.shape, F32)
        for g, h in enumerate(heads):
            s_ref[0, g] = qk(h, 0)

        def att_body(it, carry, kv=kv, heads=heads):
            ms, ls = list(carry[0]), list(carry[1])
            for si in range(2):
                c = it * 2 + si
                rows = pl.ds(pl.multiple_of(c * ATT_CHUNK, ATT_CHUNK), ATT_CHUNK)
                bias = jnp.where(sc_ref[rows, :] >= thr, 0.0, NEG_BIG)
                v_t = vt_ref[c, kv * HEAD_DIM:(kv + 1) * HEAD_DIM, :]
                for g, h in enumerate(heads):
                    s = s_ref[si, g] + bias
                    s_ref[1 - si, g] = qk(h, jnp.minimum(c + 1, last_chunk))
                    m_new = jnp.maximum(ms[g], _col_reduce(s, jnp.max))
                    alpha = jnp.exp2(ms[g] - m_new)
                    p = jnp.exp2(s - m_new)
                    ls[g] = alpha * ls[g] + _col_reduce(p, jnp.sum)
                    pv = jnp.dot(v_t, p.astype(BF16), preferred_element_type=F32)
                    acc_ref[g] = alpha * acc_ref[g] + pv
                    ms[g] = m_new
            return tuple(ms), tuple(ls)

        m0 = tuple(jnp.full((1, Q_TILE), NEG_BIG, F32) for _ in heads)
        l0 = tuple(jnp.zeros((1, Q_TILE), F32) for _ in heads)
        _, ls = lax.fori_loop(0, n_it, att_body, (m0, l0))
        for g, h in enumerate(heads):
            o_ref[:, h * HEAD_DIM:(h + 1) * HEAD_DIM] = (acc_ref[g] / ls[g]).T.astype(o_ref.dtype)


def _dsa(q_t, k_t, v_t, q_idx_t, k_idx, w_idx_t, out_dtype):
    l = q_t.shape[1]
    topk = min(INDEX_TOPK, l // 4)
    assert l % (2 * ATT_CHUNK) == 0
    n_att_chunks = l // ATT_CHUNK
    k3 = k_t.reshape(ATT_KV_HEADS, HEAD_DIM, l).transpose(0, 2, 1)
    vt = v_t.reshape(KV_WIDTH, n_att_chunks, ATT_CHUNK).transpose(1, 0, 2)
    kern = functools.partial(_dsa_kernel, topk=topk, n_att_chunks=n_att_chunks)
    return pl.pallas_call(
        kern,
        grid=(l // Q_TILE,),
        in_specs=[
            pl.BlockSpec((ATT_WIDTH, Q_TILE), lambda i: (0, i)),
            pl.BlockSpec((IDX_HEADS * IDX_DIM, Q_TILE), lambda i: (0, i)),
            pl.BlockSpec((IDX_HEADS, Q_TILE), lambda i: (0, i)),
            _resident((l, IDX_DIM)),
            _resident((ATT_KV_HEADS, l, HEAD_DIM)),
            _resident((n_att_chunks, KV_WIDTH, ATT_CHUNK)),
        ],
        out_specs=pl.BlockSpec((Q_TILE, ATT_WIDTH), lambda i: (i, 0)),
        out_shape=jax.ShapeDtypeStruct((l, ATT_WIDTH), out_dtype),
        scratch_shapes=[
            pltpu.VMEM((l, Q_TILE), F32),
            pltpu.VMEM((2, ATT_HEADS // ATT_KV_HEADS, ATT_CHUNK, Q_TILE), F32),
            pltpu.VMEM((ATT_HEADS // ATT_KV_HEADS, HEAD_DIM, Q_TILE), F32),
        ],
        compiler_params=_params(("arbitrary",)),
        name="dsa",
    )(q_t, q_idx_t, w_idx_t, k_idx, k3, vt)


def _s5_kernel(x_ref, r0_ref, per_ref, pei_ref, qre_ref, qim_ref, lam_ref, y_ref,
               mt_ref, er_ref, ei_ref, sr_ref, si_ref):
    r0 = r0_ref[0]
    lane = lax.broadcasted_iota(jnp.int32, r0.shape, 1)
    for s in range(SSM_T):
        shifted = r0 if s == 0 else pltpu.roll(r0, s * SSM_GROUP_CH, 1)
        mt_ref[s * SSM_GROUP_CH:(s + 1) * SSM_GROUP_CH, :] = jnp.where(
            lane >= s * SSM_GROUP_CH, shifted, 0.0).astype(BF16)
    xb = x_ref[0]
    er_ref[...] = jnp.dot(xb, per_ref[0], preferred_element_type=F32)
    ei_ref[...] = jnp.dot(xb, pei_ref[0], preferred_element_type=F32)
    lam_r = lam_ref[0, 0:1, :]
    lam_i = lam_ref[0, 1:2, :]
    n_chunks = xb.shape[0]

    def scan_body(c, carry):
        s_r, s_i = carry
        sr_ref[pl.ds(c, 1), :] = s_r
        si_ref[pl.ds(c, 1), :] = s_i
        e_r = er_ref[pl.ds(c, 1), :]
        e_i = ei_ref[pl.ds(c, 1), :]
        return (lam_r * s_r - lam_i * s_i + e_r, lam_r * s_i + lam_i * s_r + e_i)

    zero = jnp.zeros((1, SSM_STATE), F32)
    lax.fori_loop(0, n_chunks, scan_body, (zero, zero))
    y = jnp.dot(xb, mt_ref[...], preferred_element_type=F32)
    y = y + jnp.dot(sr_ref[...].astype(BF16), qre_ref[0], preferred_element_type=F32)
    y = y + jnp.dot(si_ref[...].astype(BF16), qim_ref[0], preferred_element_type=F32)
    y_ref[0] = y.astype(y_ref.dtype)


def _s5(u, a_re, a_im, log_dt, b_re, b_im, c_re, c_im):
    l = u.shape[0]
    t, c, p, g = SSM_T, SSM_GROUP_CH, SSM_STATE, SSM_GROUPS
    n_chunks = l // t
    width = t * c
    dt = jnp.exp(log_dt.astype(F32))[:, None]
    ar, ai = a_re.astype(F32), a_im.astype(F32)
    mag = jnp.exp(ar * dt)
    lb_re, lb_im = mag * jnp.cos(ai * dt), mag * jnp.sin(ai * dt)
    den = ar * ar + ai * ai
    coef_re = ((lb_re - 1.0) * ar + lb_im * ai) / den
    coef_im = (lb_im * ar - (lb_re - 1.0) * ai) / den
    br, bi = b_re.astype(F32), b_im.astype(F32)
    bb_re = coef_re[..., None] * br - coef_im[..., None] * bi
    bb_im = coef_re[..., None] * bi + coef_im[..., None] * br
    cr, ci = c_re.astype(F32), c_im.astype(F32)
    pr, pi = jnp.ones((g, 1, p), F32), jnp.zeros((g, 1, p), F32)
    sr, si = lb_re[:, None, :], lb_im[:, None, :]
    while pr.shape[1] < t + 1:
        pr, pi = (jnp.concatenate([pr, pr * sr - pi * si], axis=1),
                  jnp.concatenate([pi, pr * si + pi * sr], axis=1))
        sr, si = sr * sr - si * si, 2.0 * sr * si
    pr, pi = pr[:, :t + 1], pi[:, :t + 1]
    gc_re = cr[:, :, None, :] * pr[:, None] - ci[:, :, None, :] * pi[:, None]
    gc_im = cr[:, :, None, :] * pi[:, None] + ci[:, :, None, :] * pr[:, None]
    hp = lax.Precision.HIGHEST
    kj = (jnp.einsum("gojp,gpi->gijo", gc_re[:, :, :t], bb_re, precision=hp)
          - jnp.einsum("gojp,gpi->gijo", gc_im[:, :, :t], bb_im, precision=hp))
    r0 = kj.reshape(g, c, width)
    rr, ri = pr[:, :t][:, ::-1][:, :, None, :], pi[:, :t][:, ::-1][:, :, None, :]
    tb_re, tb_im = bb_re.transpose(0, 2, 1)[:, None], bb_im.transpose(0, 2, 1)[:, None]
    pe_re = (rr * tb_re - ri * tb_im).reshape(g, width, p)
    pe_im = (rr * tb_im + ri * tb_re).reshape(g, width, p)
    q_re = gc_re[:, :, 1:t + 1].transpose(0, 3, 2, 1).reshape(g, p, width)
    q_im = gc_im[:, :, 1:t + 1].transpose(0, 3, 2, 1).reshape(g, p, width)
    lam_pack = jnp.stack([pr[:, t], pi[:, t]], axis=1)
    x = u.astype(BF16).reshape(n_chunks, t, g, c).transpose(2, 0, 1, 3).reshape(g, n_chunks, width)
    grp3 = lambda a, b: pl.BlockSpec((1, a, b), lambda i: (i, 0, 0))
    y = pl.pallas_call(
        _s5_kernel,
        grid=(g,),
        in_specs=[grp3(n_chunks, width), grp3(c, width), grp3(width, p), grp3(width, p),
                  grp3(p, width), grp3(p, width), grp3(2, p)],
        out_specs=grp3(n_chunks, width),
        out_shape=jax.ShapeDtypeStruct((g, n_chunks, width), BF16),
        scratch_shapes=[
            pltpu.VMEM((width, width), BF16),
            pltpu.VMEM((n_chunks, p), F32),
            pltpu.VMEM((n_chunks, p), F32),
            pltpu.VMEM((n_chunks, p), F32),
            pltpu.VMEM((n_chunks, p), F32),
        ],
        compiler_params=_params(("parallel",)),
        name="s5",
    )(x, r0, pe_re.astype(BF16), pe_im.astype(BF16), q_re.astype(BF16), (-q_im).astype(BF16), lam_pack)
    return y.reshape(g, n_chunks, t, c).transpose(1, 2, 0, 3).reshape(l, SSM_WIDTH)


def _gelu_tanh(x):
    return 0.5 * x * (1.0 + jnp.tanh(math.sqrt(2.0 / math.pi) * (x + 0.044715 * (x * x * x))))


def _out_kernel(att_ref, ag_ref, ys_ref, u_ref, sg_ref, gate_ref, x_ref, d_ref, watt_ref, wglu_ref, bglu_ref,
                wssm_ref, wout_ref, gfin_ref, o_ref):
    a = (att_ref[...].astype(F32) * ag_ref[...].astype(F32)).astype(BF16)
    y_a = jnp.dot(a, watt_ref[...], preferred_element_type=F32)
    s = _gelu_tanh(ys_ref[...].astype(F32) + d_ref[...] * u_ref[...])
    z = jnp.dot(s.astype(BF16), wglu_ref[...], preferred_element_type=F32) + bglu_ref[...]
    s = s * jax.nn.sigmoid(z)
    y_b = jnp.dot((s * sg_ref[...]).astype(BF16), wssm_ref[...], preferred_element_type=F32)
    gates = gate_ref[...].astype(F32)
    mixed = gates[:, :D_MODEL] * y_a + gates[:, D_MODEL:] * y_b
    r = x_ref[...] + jnp.dot(mixed.astype(BF16), wout_ref[...], preferred_element_type=F32)
    y = r * lax.rsqrt(jnp.mean(r * r, axis=-1, keepdims=True) + EPS)
    o_ref[...] = (y * gfin_ref[...]).astype(o_ref.dtype)


def _out_stage(att, att_gate, y_ssm, u, ssm_gate, gates, x, d_skip, w_att_out, w_glu, b_glu, w_ssm_out, w_out,
               g_final, tm=256):
    l = x.shape[0]
    row = lambda n: pl.BlockSpec((tm, n), lambda i: (i, 0))
    return pl.pallas_call(
        _out_kernel,
        grid=(l // tm,),
        in_specs=[row(ATT_WIDTH), row(ATT_WIDTH), row(SSM_WIDTH), row(SSM_WIDTH), row(SSM_WIDTH),
                  row(N_BRANCH * D_MODEL), row(D_MODEL), _resident((1, SSM_WIDTH)),
                  _resident((ATT_WIDTH, D_MODEL)), _resident((SSM_WIDTH, SSM_WIDTH)), _resident((1, SSM_WIDTH)),
                  _resident((SSM_WIDTH, D_MODEL)), _resident((D_MODEL, D_MODEL)), _resident((1, D_MODEL))],
        out_specs=row(D_MODEL),
        out_shape=jax.ShapeDtypeStruct((l, D_MODEL), x.dtype),
        compiler_params=_params(("parallel",)),
        name="out_stage",
    )(att, att_gate, y_ssm, u, ssm_gate, gates, x, d_skip.astype(F32).reshape(1, -1), w_att_out.astype(BF16),
      w_glu.astype(BF16), b_glu.reshape(1, -1), w_ssm_out.astype(BF16), w_out.astype(BF16), g_final.reshape(1, -1))


def _layer(x, positions, g_norm, w_in, b_merge, a_re, a_im, log_dt, ssm_b_re, ssm_b_im, ssm_c_re, ssm_c_im,
           ssm_d, w_glu, b_glu, w_att_out, w_ssm_out, w_out, g_out):
    off = [0]
    for s in SPLITS:
        off.append(off[-1] + s)
    head_end = off[7] + LANES - IDX_DIM - IDX_HEADS
    w_qkv = w_in[:, :off[3]].astype(BF16)
    mid0 = off[3] - ATT_WIDTH
    w_mid = w_in[:, mid0:head_end].astype(BF16)
    w_tail = w_in[:, off[7]:].astype(BF16)
    rope_att = _rope_tables(positions, HEAD_DIM)
    rope_idx = _rope_tables(positions, IDX_DIM)
    h = _rmsnorm(x, g_norm, BF16)

    q_scale = HEAD_DIM ** -0.5 * LOG2E
    q_t = _project("q", h, w_qkv, off[0], ATT_WIDTH, ((0, ATT_WIDTH, "rope128", q_scale),), BF16, rope_att,
                   transpose_out=True)
    kv_t = _project("kv", h, w_qkv, off[1], 2 * KV_WIDTH,
                    ((0, KV_WIDTH, "rope128", 1.0), (KV_WIDTH, KV_WIDTH, "none", 1.0)), BF16, rope_att,
                    transpose_out=True)
    att_gate = _project("att_gate", h, w_mid, off[3] - mid0, ATT_WIDTH, ((0, ATT_WIDTH, "silu", 1.0),), BF16)
    q_idx_t = _project("q_idx", h, w_mid, off[4] - mid0, IDX_HEADS * IDX_DIM,
                       ((0, IDX_HEADS * IDX_DIM, "rope64", 1.0),), BF16, rope_idx, transpose_out=True)
    misc_t = _project("idx_misc", h, w_mid, off[5] - mid0, LANES,
                      ((0, LANES, "idx_misc", IDX_HEADS ** -0.5 * IDX_DIM ** -0.5),), F32, rope_idx,
                      transpose_out=True)
    u = _project("u", h, w_tail, 0, SSM_WIDTH, ((0, SSM_WIDTH, "none", 1.0),), F32)
    ssm_gate = _project("ssm_gate", h, w_tail, SSM_WIDTH, SSM_WIDTH, ((0, SSM_WIDTH, "silu", 1.0),), BF16)
    gates = _project("merge", h, w_tail, 2 * SSM_WIDTH, N_BRANCH * D_MODEL, ((0, 1024, "sigmoid", 1.0),), BF16,
                     bias=b_merge, tm=1024, tn=1024)

    k_idx = misc_t[:IDX_DIM].T.astype(BF16)
    w_idx_t = misc_t[IDX_DIM:IDX_DIM + IDX_HEADS]
    att = _dsa(q_t, kv_t[:KV_WIDTH], kv_t[KV_WIDTH:], q_idx_t, k_idx, w_idx_t, BF16)
    y_ssm = _s5(u, a_re, a_im, log_dt, ssm_b_re, ssm_b_im, ssm_c_re, ssm_c_im)
    return _out_stage(att, att_gate, y_ssm, u, ssm_gate, gates, x, ssm_d, w_att_out, w_glu, b_glu, w_ssm_out, w_out,
                      g_out)


def kernel(x, positions, g_norm, w_in, b_merge, a_re, a_im, log_dt, ssm_b_re, ssm_b_im, ssm_c_re, ssm_c_im,
           ssm_d, w_glu, b_glu, w_att_out, w_ssm_out, w_out, g_final):
    depth = w_in.shape[0]
    assert depth == 1 and x.shape[0] == 1, "single layer, single sequence"
    y = _layer(x[0], positions[0], g_norm[0], w_in[0], b_merge[0], a_re[0], a_im[0], log_dt[0], ssm_b_re[0],
               ssm_b_im[0], ssm_c_re[0], ssm_c_im[0], ssm_d[0], w_glu[0], b_glu[0], w_att_out[0], w_ssm_out[0],
               w_out[0], g_final)
    return y[None]
```

```python
import functools
import math

import jax
import jax.numpy as jnp
from jax import lax
from jax.experimental import pallas as pl
from jax.experimental.pallas import tpu as pltpu

D_MODEL = 2048
CHUNK = 64
EPS = 1e-6
ATT_HEADS = 8
ATT_KV_HEADS = 2
HEAD_DIM = 128
ATT_WIDTH = ATT_HEADS * HEAD_DIM
KV_WIDTH = ATT_KV_HEADS * HEAD_DIM
IDX_HEADS = 16
IDX_DIM = 64
INDEX_TOPK = 256
ROPE_THETA = 500000.0
ROPE_FRACTION = 4
SSM_GROUP_CH = 16
SSM_GROUPS = 32
SSM_WIDTH = SSM_GROUPS * SSM_GROUP_CH
SSM_STATE = 64
N_BRANCH = 2
SPLITS = (ATT_WIDTH, KV_WIDTH, KV_WIDTH, ATT_WIDTH, IDX_HEADS * IDX_DIM, IDX_DIM, IDX_HEADS,
          SSM_WIDTH, SSM_WIDTH, N_BRANCH * D_MODEL)

LANES = 128
ROPE_FREQS = HEAD_DIM // ROPE_FRACTION // 2
VMEM_LIMIT = 56 * 1024 * 1024

Q_TILE = 256
KEY_TILE = 256
ATT_CHUNK = 512
BLIND_PASSES = 18
SSM_T = 64
LOG2E = 1.4426950408889634
NEG_BIG = -1e30

F32 = jnp.float32
BF16 = jnp.bfloat16


def _params(semantics):
    return pltpu.CompilerParams(dimension_semantics=semantics, vmem_limit_bytes=VMEM_LIMIT)


def _resident(shape):
    zeros = (0,) * len(shape)
    return pl.BlockSpec(shape, lambda *_: zeros, pipeline_mode=pl.Buffered(1))


def _rope_kernel(pos_ref, freq_ref, cos_ref, sin_ref):
    ang = freq_ref[...] * pos_ref[...].astype(F32)
    cos_ref[...] = jnp.cos(ang)
    sin_ref[...] = jnp.sin(ang)


def _rope_tables(positions, head_dim):
    l = positions.shape[0]
    rd = head_dim // ROPE_FRACTION
    half = rd // 2
    inv_freq = jnp.exp(-math.log(ROPE_THETA) * jnp.arange(half, dtype=F32) * (2.0 / rd))
    return pl.pallas_call(
        _rope_kernel,
        out_shape=(jax.ShapeDtypeStruct((half, l), F32),) * 2,
        name="rope_tables_%d" % head_dim,
    )(positions.reshape(1, l), inv_freq.reshape(half, 1))


def _rmsnorm_kernel(x_ref, g_ref, o_ref):
    x = x_ref[...]
    y = x * lax.rsqrt(jnp.mean(x * x, axis=-1, keepdims=True) + EPS)
    o_ref[...] = (y * g_ref[...]).astype(o_ref.dtype)


def _rmsnorm(x, g, out_dtype, tm=512):
    l, d = x.shape
    return pl.pallas_call(
        _rmsnorm_kernel,
        grid=(l // tm,),
        in_specs=[pl.BlockSpec((tm, d), lambda i: (i, 0)), pl.BlockSpec((1, d), lambda i: (0, 0))],
        out_specs=pl.BlockSpec((tm, d), lambda i: (i, 0)),
        out_shape=jax.ShapeDtypeStruct((l, d), out_dtype),
        compiler_params=_params(("parallel",)),
        name="rmsnorm",
    )(x, g.reshape(1, d))


def _silu(x):
    return x * jax.nn.sigmoid(x)


def _rotate_rows(xt, cos, sin, head_dim):
    half = head_dim // ROPE_FRACTION // 2
    pieces = []
    for base in range(0, LANES, head_dim):
        top, bot = xt[base:base + half], xt[base + half:base + 2 * half]
        pieces += [top * cos - bot * sin, bot * cos + top * sin, xt[base + 2 * half:base + head_dim]]
    return jnp.concatenate(pieces, axis=0)


def _proj_kernel(*refs, segments, has_rope, has_bias, transpose_out):
    h_ref, w_ref = refs[0], refs[1]
    pos = 2
    if has_bias:
        bias_ref = refs[pos]
        pos += 1
    if has_rope:
        cos, sin = refs[pos][...], refs[pos + 1][...]
        pos += 2
    o_ref = refs[pos]
    acc = jnp.dot(h_ref[...], w_ref[...], preferred_element_type=F32)
    if has_bias:
        acc = acc + bias_ref[...]
    for start, size, kind, scale in segments:
        for b in range(size // LANES):
            lo = start + b * LANES
            xb = acc[:, lo:lo + LANES]
            if kind == "silu":
                xb = _silu(xb)
            elif kind == "sigmoid":
                xb = jax.nn.sigmoid(xb)
            if not transpose_out:
                assert kind in ("none", "silu", "sigmoid") and scale == 1.0
                o_ref[:, lo:lo + LANES] = xb.astype(o_ref.dtype)
                continue
            xt = xb.T
            if kind == "rope128":
                xt = _rotate_rows(xt, cos, sin, HEAD_DIM)
            elif kind == "rope64":
                xt = _rotate_rows(xt, cos, sin, IDX_DIM)
            elif kind == "idx_misc":
                xt = jnp.concatenate([_rotate_rows(xt, cos, sin, IDX_DIM)[:IDX_DIM], xt[IDX_DIM:] * scale], axis=0)
            if kind != "idx_misc" and scale != 1.0:
                xt = xt * scale
            o_ref[lo:lo + LANES, :] = xt.astype(o_ref.dtype)


def _project(name, h, w, col0, n, segments, out_dtype, rope=None, bias=None, tm=512, tn=None,
             transpose_out=False):
    l, d = h.shape
    tn = n if tn is None else tn
    assert col0 % tn == 0 and n % tn == 0
    jb = col0 // tn
    if transpose_out:
        out_spec = pl.BlockSpec((tn, tm), lambda i, j: (j, i))
        out_shape = jax.ShapeDtypeStruct((n, l), out_dtype)
    else:
        out_spec = pl.BlockSpec((tm, tn), lambda i, j: (i, j))
        out_shape = jax.ShapeDtypeStruct((l, n), out_dtype)
    in_specs = [pl.BlockSpec((tm, d), lambda i, j: (i, 0)), pl.BlockSpec((d, tn), lambda i, j: (0, jb + j))]
    args = [h, w]
    if bias is not None:
        in_specs.append(pl.BlockSpec((1, tn), lambda i, j: (0, j)))
        args.append(bias.reshape(1, n))
    if rope is not None:
        assert transpose_out
        in_specs += [pl.BlockSpec((rope[0].shape[0], tm), lambda i, j: (0, i))] * 2
        args += list(rope)
    kern = functools.partial(_proj_kernel, segments=segments, has_rope=rope is not None, has_bias=bias is not None,
                             transpose_out=transpose_out)
    return pl.pallas_call(
        kern,
        grid=(l // tm, n // tn),
        in_specs=in_specs,
        out_specs=out_spec,
        out_shape=out_shape,
        compiler_params=_params(("parallel", "arbitrary")),
        name="in_proj_" + name,
    )(*args)


def _col_reduce(x, op):
    r, q = x.shape
    y = op(x.reshape(r // 64, 64, q), axis=0) if r > 64 else x
    y = op(y.reshape(8, 8, q), axis=0)
    return op(y, axis=0, keepdims=True)


def _dsa_kernel(qt_ref, qit_ref, wt_ref, kidx_ref, k_ref, vt_ref, o_ref,
                sc_ref, s_ref, acc_ref, *, topk, n_att_chunks):
    i = pl.program_id(0)
    n_blk = i + 1
    sub = 8
    grp = ATT_HEADS // ATT_KV_HEADS
    inf = float("inf")

    def score_block(t, carry, diagonal):
        smin, smax = carry
        rows = pl.ds(pl.multiple_of(t * KEY_TILE, KEY_TILE), KEY_TILE)
        kt = kidx_ref[rows, :]
        acc = jnp.zeros((KEY_TILE, Q_TILE), F32)
        for h in range(IDX_HEADS):
            logit = jnp.dot(kt, qit_ref[h * IDX_DIM:(h + 1) * IDX_DIM, :], preferred_element_type=F32)
            acc = acc + wt_ref[h:h + 1, :] * jnp.maximum(logit, 0.0)
        lo_src = acc
        if diagonal:
            r_chunk = lax.broadcasted_iota(jnp.int32, acc.shape, 0) // CHUNK
            q_chunk = lax.broadcasted_iota(jnp.int32, acc.shape, 1) // CHUNK
            lo_src = jnp.where(r_chunk <= q_chunk, acc, inf)
            acc = jnp.where(r_chunk <= q_chunk, acc, -inf)
        sc_ref[rows, :] = acc
        return (jnp.minimum(smin, jnp.min(lo_src.reshape(KEY_TILE // sub, sub, Q_TILE), axis=0)),
                jnp.maximum(smax, jnp.max(acc.reshape(KEY_TILE // sub, sub, Q_TILE), axis=0)))

    carry = lax.fori_loop(0, i, lambda t, c: score_block(t, c, False),
                          (jnp.full((sub, Q_TILE), inf, F32), jnp.full((sub, Q_TILE), -inf, F32)))
    smin, smax = score_block(i, carry, True)
    smin = jnp.min(smin, axis=0, keepdims=True)
    smax = jnp.max(smax, axis=0, keepdims=True)

    blk_per_it = 2 * ATT_CHUNK // KEY_TILE
    n_it = (n_blk + blk_per_it - 1) // blk_per_it

    def fill_body(t, carry):
        sc_ref[pl.ds(pl.multiple_of(t * KEY_TILE, KEY_TILE), KEY_TILE), :] = jnp.full(
            (KEY_TILE, Q_TILE), -inf, F32)
        return carry

    lax.fori_loop(n_blk, n_it * blk_per_it, fill_body, 0)

    def count_ge(cand):
        def count_body(t, accs):
            blk = sc_ref[pl.ds(pl.multiple_of(t * KEY_TILE, KEY_TILE), KEY_TILE), :]
            hit = (blk >= cand).astype(jnp.int32)
            return accs + jnp.sum(hit.reshape(4, KEY_TILE // (4 * sub), sub, Q_TILE), axis=1)

        accs = lax.fori_loop(0, n_blk, count_body, jnp.zeros((4, sub, Q_TILE), jnp.int32))
        return jnp.sum(jnp.sum(accs, axis=0), axis=0, keepdims=True)

    lane_chunk = lax.broadcasted_iota(jnp.int32, (1, Q_TILE), 1) // CHUNK
    n_adm = (i * (Q_TILE // CHUNK) + lane_chunk + 1) * CHUNK
    hi0 = smax + (jnp.abs(smax) * 1e-6 + 1e-30)

    def bis_step(state):
        lo, hi, cnt_lo, cnt_hi, live_i = state
        mid = lo + 0.5 * (hi - lo)
        live = (live_i > 0) & (mid > lo) & (mid < hi)
        cnt = count_ge(mid)
        ge = cnt >= topk
        lo = jnp.where(live & ge, mid, lo)
        hi = jnp.where(live & ~ge, mid, hi)
        cnt_lo = jnp.where(live & ge, cnt, cnt_lo)
        cnt_hi = jnp.where(live & ~ge, cnt, cnt_hi)
        return lo, hi, cnt_lo, cnt_hi, (live & (cnt_lo > topk)).astype(jnp.int32)

    def bis_body(carry):
        state = bis_step(carry[1:])
        return (jnp.max(state[4]),) + state

    state = lax.fori_loop(0, BLIND_PASSES, lambda _, st: bis_step(st),
                          (smin, hi0, n_adm, jnp.zeros((1, Q_TILE), jnp.int32), (n_adm > topk).astype(jnp.int32)))
    _, thr, _, cnt_lo, cnt_hi, _ = lax.while_loop(lambda c: c[0] > 0, bis_body, (jnp.max(state[4]),) + state)

    tied = cnt_lo > topk

    @pl.when(jnp.max(tied.astype(jnp.int32)) > 0)
    def _():
        keep = (topk - cnt_hi).astype(F32)
        tri = (lax.broadcasted_iota(jnp.int32, (KEY_TILE, KEY_TILE), 0)
               >= lax.broadcasted_iota(jnp.int32, (KEY_TILE, KEY_TILE), 1)).astype(BF16)

        def fix_body(t, before):
            rows = pl.ds(pl.multiple_of(t * KEY_TILE, KEY_TILE), KEY_TILE)
            blk = sc_ref[rows, :]
            eq = (blk == thr) & tied
            eq_f = eq.astype(F32)
            rank = jnp.dot(tri, eq_f.astype(BF16), preferred_element_type=F32) - 1.0 + before
            sc_ref[rows, :] = jnp.where(eq & (rank >= keep), -inf, blk)
            return before + jnp.sum(eq_f, axis=0, keepdims=True)

        lax.fori_loop(0, n_blk, fix_body, jnp.zeros((1, Q_TILE), F32))

    last_chunk = n_att_chunks - 1

    def qk(h, c):
        rows = pl.ds(pl.multiple_of(c * ATT_CHUNK, ATT_CHUNK), ATT_CHUNK)
        return jnp.dot(k_ref[h // grp, rows, :], qt_ref[h * HEAD_DIM:(h + 1) * HEAD_DIM, :],
                       preferred_element_type=F32)

    for kv in range(ATT_KV_HEADS):
        heads = range(kv * grp, (kv + 1) * grp)
        acc_ref[...] = jnp.zeros(acc_ref.shape, F32)
        for g, h in enumerate(heads):
            s_ref[0, g] = qk(h, 0)

        def att_body(it, carry, kv=kv, heads=heads):
            ms, ls = list(carry[0]), list(carry[1])
            for si in range(2):
                c = it * 2 + si
                rows = pl.ds(pl.multiple_of(c * ATT_CHUNK, ATT_CHUNK), ATT_CHUNK)
                bias = jnp.where(sc_ref[rows, :] >= thr, 0.0, NEG_BIG)
                v_t = vt_ref[c, kv * HEAD_DIM:(kv + 1) * HEAD_DIM, :]
                for g, h in enumerate(heads):
                    s = s_ref[si, g] + bias
                    s_ref[1 - si, g] = qk(h, jnp.minimum(c + 1, last_chunk))
                    m_new = jnp.maximum(ms[g], _col_reduce(s, jnp.max))
                    alpha = jnp.exp2(ms[g] - m_new)
                    p = jnp.exp2(s - m_new)
                    ls[g] = alpha * ls[g] + _col_reduce(p, jnp.sum)
                    pv = jnp.dot(v_t, p.astype(BF16), preferred_element_type=F32)
                    acc_ref[g] = alpha * acc_ref[g] + pv
                    ms[g] = m_new
            return tuple(ms), tuple(ls)

        m0 = tuple(jnp.full((1, Q_TILE), NEG_BIG, F32) for _ in heads)
        l0 = tuple(jnp.zeros((1, Q_TILE), F32) for _ in heads)
        _, ls = lax.fori_loop(0, n_it, att_body, (m0, l0))
        for g, h in enumerate(heads):
            o_ref[:, h * HEAD_DIM:(h + 1) * HEAD_DIM] = (acc_ref[g] / ls[g]).T.astype(o_ref.dtype)


def _dsa(q_t, k_t, v_t, q_idx_t, k_idx, w_idx_t, out_dtype):
    l = q_t.shape[1]
    topk = min(INDEX_TOPK, l // 4)
    assert l % (2 * ATT_CHUNK) == 0
    n_att_chunks = l // ATT_CHUNK
    k3 = k_t.reshape(ATT_KV_HEADS, HEAD_DIM, l).transpose(0, 2, 1)
    vt = v_t.reshape(KV_WIDTH, n_att_chunks, ATT_CHUNK).transpose(1, 0, 2)
    kern = functools.partial(_dsa_kernel, topk=topk, n_att_chunks=n_att_chunks)
    return pl.pallas_call(
        kern,
        grid=(l // Q_TILE,),
        in_specs=[
            pl.BlockSpec((ATT_WIDTH, Q_TILE), lambda i: (0, i)),
            pl.BlockSpec((IDX_HEADS * IDX_DIM, Q_TILE), lambda i: (0, i)),
            pl.BlockSpec((IDX_HEADS, Q_TILE), lambda i: (0, i)),
            _resident((l, IDX_DIM)),
            _resident((ATT_KV_HEADS, l, HEAD_DIM)),
            _resident((n_att_chunks, KV_WIDTH, ATT_CHUNK)),
        ],
        out_specs=pl.BlockSpec((Q_TILE, ATT_WIDTH), lambda i: (i, 0)),
        out_shape=jax.ShapeDtypeStruct((l, ATT_WIDTH), out_dtype),
        scratch_shapes=[
            pltpu.VMEM((l, Q_TILE), F32),
            pltpu.VMEM((2, ATT_HEADS // ATT_KV_HEADS, ATT_CHUNK, Q_TILE), F32),
            pltpu.VMEM((ATT_HEADS // ATT_KV_HEADS, HEAD_DIM, Q_TILE), F32),
        ],
        compiler_params=_params(("arbitrary",)),
        name="dsa",
    )(q_t, q_idx_t, w_idx_t, k_idx, k3, vt)


def _s5_kernel(x_ref, r0_ref, per_ref, pei_ref, qre_ref, qim_ref, lam_ref, y_ref,
               mt_ref, er_ref, ei_ref, sr_ref, si_ref):
    r0 = r0_ref[0]
    lane = lax.broadcasted_iota(jnp.int32, r0.shape, 1)
    for s in range(SSM_T):
        shifted = r0 if s == 0 else pltpu.roll(r0, s * SSM_GROUP_CH, 1)
        mt_ref[s * SSM_GROUP_CH:(s + 1) * SSM_GROUP_CH, :] = jnp.where(
            lane >= s * SSM_GROUP_CH, shifted, 0.0).astype(BF16)
    xb = x_ref[0]
    er_ref[...] = jnp.dot(xb, per_ref[0], preferred_element_type=F32)
    ei_ref[...] = jnp.dot(xb, pei_ref[0], preferred_element_type=F32)
    lam_r = lam_ref[0, 0:1, :]
    lam_i = lam_ref[0, 1:2, :]
    n_chunks = xb.shape[0]

    def scan_body(c, carry):
        s_r, s_i = carry
        sr_ref[pl.ds(c, 1), :] = s_r
        si_ref[pl.ds(c, 1), :] = s_i
        e_r = er_ref[pl.ds(c, 1), :]
        e_i = ei_ref[pl.ds(c, 1), :]
        return (lam_r * s_r - lam_i * s_i + e_r, lam_r * s_i + lam_i * s_r + e_i)

    zero = jnp.zeros((1, SSM_STATE), F32)
    lax.fori_loop(0, n_chunks, scan_body, (zero, zero))
    y = jnp.dot(xb, mt_ref[...], preferred_element_type=F32)
    y = y + jnp.dot(sr_ref[...].astype(BF16), qre_ref[0], preferred_element_type=F32)
    y = y + jnp.dot(si_ref[...].astype(BF16), qim_ref[0], preferred_element_type=F32)
    y_ref[0] = y.astype(y_ref.dtype)


def _s5(u, a_re, a_im, log_dt, b_re, b_im, c_re, c_im):
    l = u.shape[0]
    t, c, p, g = SSM_T, SSM_GROUP_CH, SSM_STATE, SSM_GROUPS
    n_chunks = l // t
    width = t * c
    dt = jnp.exp(log_dt.astype(F32))[:, None]
    ar, ai = a_re.astype(F32), a_im.astype(F32)
    mag = jnp.exp(ar * dt)
    lb_re, lb_im = mag * jnp.cos(ai * dt), mag * jnp.sin(ai * dt)
    den = ar * ar + ai * ai
    coef_re = ((lb_re - 1.0) * ar + lb_im * ai) / den
    coef_im = (lb_im * ar - (lb_re - 1.0) * ai) / den
    br, bi = b_re.astype(F32), b_im.astype(F32)
    bb_re = coef_re[..., None] * br - coef_im[..., None] * bi
    bb_im = coef_re[..., None] * bi + coef_im[..., None] * br
    cr, ci = c_re.astype(F32), c_im.astype(F32)
    pr, pi = jnp.ones((g, 1, p), F32), jnp.zeros((g, 1, p), F32)
    sr, si = lb_re[:, None, :], lb_im[:, None, :]
    while pr.shape[1] < t + 1:
        pr, pi = (jnp.concatenate([pr, pr * sr - pi * si], axis=1),
                  jnp.concatenate([pi, pr * si + pi * sr], axis=1))
        sr, si = sr * sr - si * si, 2.0 * sr * si
    pr, pi = pr[:, :t + 1], pi[:, :t + 1]
    gc_re = cr[:, :, None, :] * pr[:, None] - ci[:, :, None, :] * pi[:, None]
    gc_im = cr[:, :, None, :] * pi[:, None] + ci[:, :, None, :] * pr[:, None]
    kj = jnp.sum(gc_re[:, :, :t, :, None] * bb_re[:, None, None] - gc_im[:, :, :t, :, None] * bb_im[:, None, None],
                 axis=3)
    r0 = kj.transpose(0, 3, 2, 1).reshape(g, c, width)
    rr, ri = pr[:, :t][:, ::-1][:, :, None, :], pi[:, :t][:, ::-1][:, :, None, :]
    tb_re, tb_im = bb_re.transpose(0, 2, 1)[:, None], bb_im.transpose(0, 2, 1)[:, None]
    pe_re = (rr * tb_re - ri * tb_im).reshape(g, width, p)
    pe_im = (rr * tb_im + ri * tb_re).reshape(g, width, p)
    q_re = gc_re[:, :, 1:t + 1].transpose(0, 3, 2, 1).reshape(g, p, width)
    q_im = gc_im[:, :, 1:t + 1].transpose(0, 3, 2, 1).reshape(g, p, width)
    lam_pack = jnp.stack([pr[:, t], pi[:, t]], axis=1)
    x = u.astype(BF16).reshape(n_chunks, t, g, c).transpose(2, 0, 1, 3).reshape(g, n_chunks, width)
    grp3 = lambda a, b: pl.BlockSpec((1, a, b), lambda i: (i, 0, 0))
    y = pl.pallas_call(
        _s5_kernel,
        grid=(g,),
        in_specs=[grp3(n_chunks, width), grp3(c, width), grp3(width, p), grp3(width, p),
                  grp3(p, width), grp3(p, width), grp3(2, p)],
        out_specs=grp3(n_chunks, width),
        out_shape=jax.ShapeDtypeStruct((g, n_chunks, width), BF16),
        scratch_shapes=[
            pltpu.VMEM((width, width), BF16),
            pltpu.VMEM((n_chunks, p), F32),
            pltpu.VMEM((n_chunks, p), F32),
            pltpu.VMEM((n_chunks, p), F32),
            pltpu.VMEM((n_chunks, p), F32),
        ],
        compiler_params=_params(("parallel",)),
        name="s5",
    )(x, r0, pe_re.astype(BF16), pe_im.astype(BF16), q_re.astype(BF16), (-q_im).astype(BF16), lam_pack)
    return y.reshape(g, n_chunks, t, c).transpose(1, 2, 0, 3).reshape(l, SSM_WIDTH)


def _gelu_tanh(x):
    return 0.5 * x * (1.0 + jnp.tanh(math.sqrt(2.0 / math.pi) * (x + 0.044715 * (x * x * x))))


def _out_kernel(att_ref, ag_ref, ys_ref, u_ref, sg_ref, gate_ref, x_ref, d_ref, watt_ref, wglu_ref, bglu_ref,
                wssm_ref, wout_ref, gfin_ref, o_ref):
    a = (att_ref[...].astype(F32) * ag_ref[...].astype(F32)).astype(BF16)
    y_a = jnp.dot(a, watt_ref[...], preferred_element_type=F32)
    s = _gelu_tanh(ys_ref[...].astype(F32) + d_ref[...] * u_ref[...])
    z = jnp.dot(s.astype(BF16), wglu_ref[...], preferred_element_type=F32) + bglu_ref[...]
    s = s * jax.nn.sigmoid(z)
    y_b = jnp.dot((s * sg_ref[...]).astype(BF16), wssm_ref[...], preferred_element_type=F32)
    gates = gate_ref[...].astype(F32)
    mixed = gates[:, :D_MODEL] * y_a + gates[:, D_MODEL:] * y_b
    r = x_ref[...] + jnp.dot(mixed.astype(BF16), wout_ref[...], preferred_element_type=F32)
    y = r * lax.rsqrt(jnp.mean(r * r, axis=-1, keepdims=True) + EPS)
    o_ref[...] = (y * gfin_ref[...]).astype(o_ref.dtype)


def _out_stage(att, att_gate, y_ssm, u_gate, gates, x, d_skip, w_att_out, w_glu, b_glu, w_ssm_out, w_out,
               g_final, tm=256):
    l = x.shape[0]
    row = lambda n: pl.BlockSpec((tm, n), lambda i: (i, 0))
    return pl.pallas_call(
        _out_kernel,
        grid=(l // tm,),
        in_specs=[row(ATT_WIDTH), row(ATT_WIDTH), row(SSM_WIDTH), row(SSM_WIDTH),
                  pl.BlockSpec((tm, SSM_WIDTH), lambda i: (i, 1)),
                  row(N_BRANCH * D_MODEL), row(D_MODEL), _resident((1, SSM_WIDTH)),
                  _resident((ATT_WIDTH, D_MODEL)), _resident((SSM_WIDTH, SSM_WIDTH)), _resident((1, SSM_WIDTH)),
                  _resident((SSM_WIDTH, D_MODEL)), _resident((D_MODEL, D_MODEL)), _resident((1, D_MODEL))],
        out_specs=row(D_MODEL),
        out_shape=jax.ShapeDtypeStruct((l, D_MODEL), x.dtype),
        compiler_params=_params(("parallel",)),
        name="out_stage",
    )(att, att_gate, y_ssm, u_gate, u_gate, gates, x, d_skip.astype(F32).reshape(1, -1), w_att_out.astype(BF16),
      w_glu.astype(BF16), b_glu.reshape(1, -1), w_ssm_out.astype(BF16), w_out.astype(BF16), g_final.reshape(1, -1))


def _layer(x, positions, g_norm, w_in, b_merge, a_re, a_im, log_dt, ssm_b_re, ssm_b_im, ssm_c_re, ssm_c_im,
           ssm_d, w_glu, b_glu, w_att_out, w_ssm_out, w_out, g_out):
    off = [0]
    for s in SPLITS:
        off.append(off[-1] + s)
    head_end = off[7] + LANES - IDX_DIM - IDX_HEADS
    w_qkv = w_in[:, :off[3]].astype(BF16)
    mid0 = off[3] - ATT_WIDTH
    w_mid = w_in[:, mid0:head_end].astype(BF16)
    w_tail = w_in[:, off[7]:].astype(BF16)
    rope_att = _rope_tables(positions, HEAD_DIM)
    rope_idx = _rope_tables(positions, IDX_DIM)
    h = _rmsnorm(x, g_norm, BF16)

    q_scale = HEAD_DIM ** -0.5 * LOG2E
    q_t = _project("q", h, w_qkv, off[0], ATT_WIDTH, ((0, ATT_WIDTH, "rope128", q_scale),), BF16, rope_att,
                   transpose_out=True)
    kv_t = _project("kv", h, w_qkv, off[1], 2 * KV_WIDTH,
                    ((0, KV_WIDTH, "rope128", 1.0), (KV_WIDTH, KV_WIDTH, "none", 1.0)), BF16, rope_att,
                    transpose_out=True)
    att_gate = _project("att_gate", h, w_mid, off[3] - mid0, ATT_WIDTH, ((0, ATT_WIDTH, "silu", 1.0),), BF16)
    q_idx_t = _project("q_idx", h, w_mid, off[4] - mid0, IDX_HEADS * IDX_DIM,
                       ((0, IDX_HEADS * IDX_DIM, "rope64", 1.0),), BF16, rope_idx, transpose_out=True)
    misc_t = _project("idx_misc", h, w_mid, off[5] - mid0, LANES,
                      ((0, LANES, "idx_misc", IDX_HEADS ** -0.5 * IDX_DIM ** -0.5),), F32, rope_idx,
                      transpose_out=True)
    u_gate = _project("ssm", h, w_tail, 0, 2 * SSM_WIDTH,
                      ((0, SSM_WIDTH, "none", 1.0), (SSM_WIDTH, SSM_WIDTH, "silu", 1.0)), F32)
    gates = _project("merge", h, w_tail, 2 * SSM_WIDTH, N_BRANCH * D_MODEL, ((0, 1024, "sigmoid", 1.0),), BF16,
                     bias=b_merge, tm=1024, tn=1024)

    k_idx = misc_t[:IDX_DIM].T.astype(BF16)
    w_idx_t = misc_t[IDX_DIM:IDX_DIM + IDX_HEADS]
    att = _dsa(q_t, kv_t[:KV_WIDTH], kv_t[KV_WIDTH:], q_idx_t, k_idx, w_idx_t, BF16)
    y_ssm = _s5(u_gate[:, :SSM_WIDTH], a_re, a_im, log_dt, ssm_b_re, ssm_b_im, ssm_c_re, ssm_c_im)
    return _out_stage(att, att_gate, y_ssm, u_gate, gates, x, ssm_d, w_att_out, w_glu, b_glu, w_ssm_out, w_out, g_out)


def kernel(x, positions, g_norm, w_in, b_merge, a_re, a_im, log_dt, ssm_b_re, ssm_b_im, ssm_c_re, ssm_c_im,
           ssm_d, w_glu, b_glu, w_att_out, w_ssm_out, w_out, g_final):
    depth = w_in.shape[0]
    assert depth == 1 and x.shape[0] == 1, "single layer, single sequence"
    y = _layer(x[0], positions[0], g_norm[0], w_in[0], b_merge[0], a_re[0], a_im[0], log_dt[0], ssm_b_re[0],
               ssm_b_im[0], ssm_c_re[0], ssm_c_im[0], ssm_d[0], w_glu[0], b_glu[0], w_att_out[0], w_ssm_out[0],
               w_out[0], g_final)
    return y[None]
```

```python
import functools
import math

import jax
import jax.numpy as jnp
from jax import lax
from jax.experimental import pallas as pl
from jax.experimental.pallas import tpu as pltpu

D_MODEL = 2048
CHUNK = 64
EPS = 1e-6
ATT_HEADS = 8
ATT_KV_HEADS = 2
HEAD_DIM = 128
ATT_WIDTH = ATT_HEADS * HEAD_DIM
KV_WIDTH = ATT_KV_HEADS * HEAD_DIM
IDX_HEADS = 16
IDX_DIM = 64
INDEX_TOPK = 256
ROPE_THETA = 500000.0
ROPE_FRACTION = 4
SSM_GROUP_CH = 16
SSM_GROUPS = 32
SSM_WIDTH = SSM_GROUPS * SSM_GROUP_CH
SSM_STATE = 64
N_BRANCH = 2
SPLITS = (ATT_WIDTH, KV_WIDTH, KV_WIDTH, ATT_WIDTH, IDX_HEADS * IDX_DIM, IDX_DIM, IDX_HEADS,
          SSM_WIDTH, SSM_WIDTH, N_BRANCH * D_MODEL)

LANES = 128
ROPE_FREQS = HEAD_DIM // ROPE_FRACTION // 2
VMEM_LIMIT = 56 * 1024 * 1024

Q_TILE = 256
KEY_TILE = 256
ATT_CHUNK = 512
BLIND_PASSES = 18
SSM_T = 64
LOG2E = 1.4426950408889634
NEG_BIG = -1e30

F32 = jnp.float32
BF16 = jnp.bfloat16


def _params(semantics):
    return pltpu.CompilerParams(dimension_semantics=semantics, vmem_limit_bytes=VMEM_LIMIT)


def _resident(shape):
    zeros = (0,) * len(shape)
    return pl.BlockSpec(shape, lambda *_: zeros, pipeline_mode=pl.Buffered(1))


def _rope_kernel(pos_ref, freq_ref, cos_ref, sin_ref):
    ang = freq_ref[...] * pos_ref[...].astype(F32)
    cos_ref[...] = jnp.cos(ang)
    sin_ref[...] = jnp.sin(ang)


def _rope_tables(positions, head_dim):
    l = positions.shape[0]
    rd = head_dim // ROPE_FRACTION
    half = rd // 2
    inv_freq = jnp.exp(-math.log(ROPE_THETA) * jnp.arange(half, dtype=F32) * (2.0 / rd))
    return pl.pallas_call(
        _rope_kernel,
        out_shape=(jax.ShapeDtypeStruct((half, l), F32),) * 2,
        name="rope_tables_%d" % head_dim,
    )(positions.reshape(1, l), inv_freq.reshape(half, 1))


def _silu(x):
    return x * jax.nn.sigmoid(x)


def _rotate_rows(xt, cos, sin, head_dim):
    half = head_dim // ROPE_FRACTION // 2
    pieces = []
    for base in range(0, LANES, head_dim):
        top, bot = xt[base:base + half], xt[base + half:base + 2 * half]
        pieces += [top * cos - bot * sin, bot * cos + top * sin, xt[base + 2 * half:base + head_dim]]
    return jnp.concatenate(pieces, axis=0)


def _proj_kernel(*refs, segments, has_rope, has_bias, has_norm, transpose_out):
    h_ref, w_ref = refs[0], refs[1]
    pos = 2
    if has_norm:
        gain_ref = refs[pos]
        pos += 1
    if has_bias:
        bias_ref = refs[pos]
        pos += 1
    if has_rope:
        cos, sin = refs[pos][...], refs[pos + 1][...]
        pos += 2
    o_ref = refs[pos]
    if has_norm:
        x = h_ref[...]
        h = (x * lax.rsqrt(jnp.mean(x * x, axis=-1, keepdims=True) + EPS) * gain_ref[...]).astype(BF16)
        refs[pos + 1][...] = h
    else:
        h = h_ref[...]
    acc = jnp.dot(h, w_ref[...], preferred_element_type=F32)
    if has_bias:
        acc = acc + bias_ref[...]
    for start, size, kind, scale in segments:
        for b in range(size // LANES):
            lo = start + b * LANES
            xb = acc[:, lo:lo + LANES]
            if kind == "silu":
                xb = _silu(xb)
            elif kind == "sigmoid":
                xb = jax.nn.sigmoid(xb)
            if not transpose_out:
                assert kind in ("none", "silu", "sigmoid") and scale == 1.0
                o_ref[:, lo:lo + LANES] = xb.astype(o_ref.dtype)
                continue
            xt = xb.T
            if kind == "rope128":
                xt = _rotate_rows(xt, cos, sin, HEAD_DIM)
            elif kind == "rope64":
                xt = _rotate_rows(xt, cos, sin, IDX_DIM)
            elif kind == "idx_misc":
                xt = jnp.concatenate([_rotate_rows(xt, cos, sin, IDX_DIM)[:IDX_DIM], xt[IDX_DIM:] * scale], axis=0)
            if kind != "idx_misc" and scale != 1.0:
                xt = xt * scale
            o_ref[lo:lo + LANES, :] = xt.astype(o_ref.dtype)


def _project(name, h, w, col0, n, segments, out_dtype, rope=None, bias=None, tm=512, tn=None,
             transpose_out=False, norm_gain=None):
    l, d = h.shape
    tn = n if tn is None else tn
    assert col0 % tn == 0 and n % tn == 0
    assert norm_gain is None or n == tn
    jb = col0 // tn
    if transpose_out:
        out_spec = pl.BlockSpec((tn, tm), lambda i, j: (j, i))
        out_shape = jax.ShapeDtypeStruct((n, l), out_dtype)
    else:
        out_spec = pl.BlockSpec((tm, tn), lambda i, j: (i, j))
        out_shape = jax.ShapeDtypeStruct((l, n), out_dtype)
    in_specs = [pl.BlockSpec((tm, d), lambda i, j: (i, 0)), pl.BlockSpec((d, tn), lambda i, j: (0, jb + j))]
    args = [h, w]
    if norm_gain is not None:
        in_specs.append(pl.BlockSpec((1, d), lambda i, j: (0, 0)))
        args.append(norm_gain.reshape(1, d))
        out_spec = (out_spec, pl.BlockSpec((tm, d), lambda i, j: (i, 0)))
        out_shape = (out_shape, jax.ShapeDtypeStruct((l, d), BF16))
    if bias is not None:
        in_specs.append(pl.BlockSpec((1, tn), lambda i, j: (0, j)))
        args.append(bias.reshape(1, n))
    if rope is not None:
        assert transpose_out
        in_specs += [pl.BlockSpec((rope[0].shape[0], tm), lambda i, j: (0, i))] * 2
        args += list(rope)
    kern = functools.partial(_proj_kernel, segments=segments, has_rope=rope is not None, has_bias=bias is not None,
                             has_norm=norm_gain is not None, transpose_out=transpose_out)
    return pl.pallas_call(
        kern,
        grid=(l // tm, n // tn),
        in_specs=in_specs,
        out_specs=out_spec,
        out_shape=out_shape,
        compiler_params=_params(("parallel", "arbitrary")),
        name="in_proj_" + name,
    )(*args)


def _col_reduce(x, op):
    r, q = x.shape
    y = op(x.reshape(r // 64, 64, q), axis=0) if r > 64 else x
    y = op(y.reshape(8, 8, q), axis=0)
    return op(y, axis=0, keepdims=True)


def _dsa_kernel(qt_ref, qit_ref, wt_ref, kidx_ref, k_ref, vt_ref, o_ref,
                sc_ref, s_ref, acc_ref, *, topk, n_att_chunks):
    i = pl.program_id(0)
    n_blk = i + 1
    sub = 8
    grp = ATT_HEADS // ATT_KV_HEADS
    inf = float("inf")

    def score_block(t, carry, diagonal):
        smin, smax = carry
        rows = pl.ds(pl.multiple_of(t * KEY_TILE, KEY_TILE), KEY_TILE)
        kt = kidx_ref[rows, :]
        acc = jnp.zeros((KEY_TILE, Q_TILE), F32)
        for h in range(IDX_HEADS):
            logit = jnp.dot(kt, qit_ref[h * IDX_DIM:(h + 1) * IDX_DIM, :], preferred_element_type=F32)
            acc = acc + wt_ref[h:h + 1, :] * jnp.maximum(logit, 0.0)
        lo_src = acc
        if diagonal:
            r_chunk = lax.broadcasted_iota(jnp.int32, acc.shape, 0) // CHUNK
            q_chunk = lax.broadcasted_iota(jnp.int32, acc.shape, 1) // CHUNK
            lo_src = jnp.where(r_chunk <= q_chunk, acc, inf)
            acc = jnp.where(r_chunk <= q_chunk, acc, -inf)
        sc_ref[rows, :] = acc
        return (jnp.minimum(smin, jnp.min(lo_src.reshape(KEY_TILE // sub, sub, Q_TILE), axis=0)),
                jnp.maximum(smax, jnp.max(acc.reshape(KEY_TILE // sub, sub, Q_TILE), axis=0)))

    carry = lax.fori_loop(0, i, lambda t, c: score_block(t, c, False),
                          (jnp.full((sub, Q_TILE), inf, F32), jnp.full((sub, Q_TILE), -inf, F32)))
    smin, smax = score_block(i, carry, True)
    smin = jnp.min(smin, axis=0, keepdims=True)
    smax = jnp.max(smax, axis=0, keepdims=True)

    blk_per_it = 2 * ATT_CHUNK // KEY_TILE
    n_it = (n_blk + blk_per_it - 1) // blk_per_it

    def fill_body(t, carry):
        sc_ref[pl.ds(pl.multiple_of(t * KEY_TILE, KEY_TILE), KEY_TILE), :] = jnp.full(
            (KEY_TILE, Q_TILE), -inf, F32)
        return carry

    lax.fori_loop(n_blk, n_it * blk_per_it, fill_body, 0)

    def count_ge(cand):
        def count_body(t, accs):
            blk = sc_ref[pl.ds(pl.multiple_of(t * KEY_TILE, KEY_TILE), KEY_TILE), :]
            hit = (blk >= cand).astype(jnp.int32)
            return accs + jnp.sum(hit.reshape(4, KEY_TILE // (4 * sub), sub, Q_TILE), axis=1)

        accs = lax.fori_loop(0, n_blk, count_body, jnp.zeros((4, sub, Q_TILE), jnp.int32))
        return jnp.sum(jnp.sum(accs, axis=0), axis=0, keepdims=True)

    lane_chunk = lax.broadcasted_iota(jnp.int32, (1, Q_TILE), 1) // CHUNK
    n_adm = (i * (Q_TILE // CHUNK) + lane_chunk + 1) * CHUNK
    hi0 = smax + (jnp.abs(smax) * 1e-6 + 1e-30)

    def bis_step(state):
        lo, hi, cnt_lo, cnt_hi, live_i = state
        mid = lo + 0.5 * (hi - lo)
        live = (live_i > 0) & (mid > lo) & (mid < hi)
        cnt = count_ge(mid)
        ge = cnt >= topk
        lo = jnp.where(live & ge, mid, lo)
        hi = jnp.where(live & ~ge, mid, hi)
        cnt_lo = jnp.where(live & ge, cnt, cnt_lo)
        cnt_hi = jnp.where(live & ~ge, cnt, cnt_hi)
        return lo, hi, cnt_lo, cnt_hi, (live & (cnt_lo > topk)).astype(jnp.int32)

    def bis_body(carry):
        state = bis_step(carry[1:])
        return (jnp.max(state[4]),) + state

    state = lax.fori_loop(0, BLIND_PASSES, lambda _, st: bis_step(st),
                          (smin, hi0, n_adm, jnp.zeros((1, Q_TILE), jnp.int32), (n_adm > topk).astype(jnp.int32)))
    _, thr, _, cnt_lo, cnt_hi, _ = lax.while_loop(lambda c: c[0] > 0, bis_body, (jnp.max(state[4]),) + state)

    tied = cnt_lo > topk

    @pl.when(jnp.max(tied.astype(jnp.int32)) > 0)
    def _():
        keep = (topk - cnt_hi).astype(F32)
        tri = (lax.broadcasted_iota(jnp.int32, (KEY_TILE, KEY_TILE), 0)
               >= lax.broadcasted_iota(jnp.int32, (KEY_TILE, KEY_TILE), 1)).astype(BF16)

        def fix_body(t, before):
            rows = pl.ds(pl.multiple_of(t * KEY_TILE, KEY_TILE), KEY_TILE)
            blk = sc_ref[rows, :]
            eq = (blk == thr) & tied
            eq_f = eq.astype(F32)
            rank = jnp.dot(tri, eq_f.astype(BF16), preferred_element_type=F32) - 1.0 + before
            sc_ref[rows, :] = jnp.where(eq & (rank >= keep), -inf, blk)
            return before + jnp.sum(eq_f, axis=0, keepdims=True)

        lax.fori_loop(0, n_blk, fix_body, jnp.zeros((1, Q_TILE), F32))

    last_chunk = n_att_chunks - 1

    def qk(h, c):
        rows = pl.ds(pl.multiple_of(c * ATT_CHUNK, ATT_CHUNK), ATT_CHUNK)
        return jnp.dot(k_ref[h // grp, rows, :], qt_ref[h * HEAD_DIM:(h + 1) * HEAD_DIM, :],
                       preferred_element_type=F32)

    for kv in range(ATT_KV_HEADS):
        heads = range(kv * grp, (kv + 1) * grp)
        acc_ref[...] = jnp.zeros(acc_ref.shape, F32)
        for g, h in enumerate(heads):
            s_ref[0, g] = qk(h, 0)

        def att_body(it, carry, kv=kv, heads=heads):
            ms, ls = list(carry[0]), list(carry[1])
            for si in range(2):
                c = it * 2 + si
                rows = pl.ds(pl.multiple_of(c * ATT_CHUNK, ATT_CHUNK), ATT_CHUNK)
                bias = jnp.where(sc_ref[rows, :] >= thr, 0.0, NEG_BIG)
                v_t = vt_ref[c, kv * HEAD_DIM:(kv + 1) * HEAD_DIM, :]
                for g, h in enumerate(heads):
                    s = s_ref[si, g] + bias
                    s_ref[1 - si, g] = qk(h, jnp.minimum(c + 1, last_chunk))
                    m_new = jnp.maximum(ms[g], _col_reduce(s, jnp.max))
                    alpha = jnp.exp2(ms[g] - m_new)
                    p = jnp.exp2(s - m_new)
                    ls[g] = alpha * ls[g] + _col_reduce(p, jnp.sum)
                    pv = jnp.dot(v_t, p.astype(BF16), preferred_element_type=F32)
                    acc_ref[g] = alpha * acc_ref[g] + pv
                    ms[g] = m_new
            return tuple(ms), tuple(ls)

        m0 = tuple(jnp.full((1, Q_TILE), NEG_BIG, F32) for _ in heads)
        l0 = tuple(jnp.zeros((1, Q_TILE), F32) for _ in heads)
        _, ls = lax.fori_loop(0, n_it, att_body, (m0, l0))
        for g, h in enumerate(heads):
            o_ref[:, h * HEAD_DIM:(h + 1) * HEAD_DIM] = (acc_ref[g] / ls[g]).T.astype(o_ref.dtype)


def _dsa(q_t, k_t, v_t, q_idx_t, k_idx, w_idx_t, out_dtype):
    l = q_t.shape[1]
    topk = min(INDEX_TOPK, l // 4)
    assert l % (2 * ATT_CHUNK) == 0
    n_att_chunks = l // ATT_CHUNK
    k3 = k_t.reshape(ATT_KV_HEADS, HEAD_DIM, l).transpose(0, 2, 1)
    vt = v_t.reshape(KV_WIDTH, n_att_chunks, ATT_CHUNK).transpose(1, 0, 2)
    kern = functools.partial(_dsa_kernel, topk=topk, n_att_chunks=n_att_chunks)
    return pl.pallas_call(
        kern,
        grid=(l // Q_TILE,),
        in_specs=[
            pl.BlockSpec((ATT_WIDTH, Q_TILE), lambda i: (0, i)),
            pl.BlockSpec((IDX_HEADS * IDX_DIM, Q_TILE), lambda i: (0, i)),
            pl.BlockSpec((IDX_HEADS, Q_TILE), lambda i: (0, i)),
            _resident((l, IDX_DIM)),
            _resident((ATT_KV_HEADS, l, HEAD_DIM)),
            _resident((n_att_chunks, KV_WIDTH, ATT_CHUNK)),
        ],
        out_specs=pl.BlockSpec((Q_TILE, ATT_WIDTH), lambda i: (i, 0)),
        out_shape=jax.ShapeDtypeStruct((l, ATT_WIDTH), out_dtype),
        scratch_shapes=[
            pltpu.VMEM((l, Q_TILE), F32),
            pltpu.VMEM((2, ATT_HEADS // ATT_KV_HEADS, ATT_CHUNK, Q_TILE), F32),
            pltpu.VMEM((ATT_HEADS // ATT_KV_HEADS, HEAD_DIM, Q_TILE), F32),
        ],
        compiler_params=_params(("arbitrary",)),
        name="dsa",
    )(q_t, q_idx_t, w_idx_t, k_idx, k3, vt)


def _s5_kernel(x_ref, bcat_ref, gcat_ref, per_ref, pei_ref, qre_ref, qim_ref, lam_ref, y_ref,
               mt_ref, er_ref, ei_ref, sr_ref, si_ref):
    r0 = jnp.dot(bcat_ref[0], gcat_ref[0], precision=lax.Precision.HIGHEST, preferred_element_type=F32)
    lane = lax.broadcasted_iota(jnp.int32, r0.shape, 1)
    for s in range(SSM_T):
        shifted = r0 if s == 0 else pltpu.roll(r0, s * SSM_GROUP_CH, 1)
        mt_ref[s * SSM_GROUP_CH:(s + 1) * SSM_GROUP_CH, :] = jnp.where(
            lane >= s * SSM_GROUP_CH, shifted, 0.0).astype(BF16)
    xb = x_ref[0]
    er_ref[...] = jnp.dot(xb, per_ref[0], preferred_element_type=F32)
    ei_ref[...] = jnp.dot(xb, pei_ref[0], preferred_element_type=F32)
    lam_r = lam_ref[0, 0:1, :]
    lam_i = lam_ref[0, 1:2, :]
    n_chunks = xb.shape[0]

    def scan_body(c, carry):
        s_r, s_i = carry
        sr_ref[pl.ds(c, 1), :] = s_r
        si_ref[pl.ds(c, 1), :] = s_i
        e_r = er_ref[pl.ds(c, 1), :]
        e_i = ei_ref[pl.ds(c, 1), :]
        return (lam_r * s_r - lam_i * s_i + e_r, lam_r * s_i + lam_i * s_r + e_i)

    zero = jnp.zeros((1, SSM_STATE), F32)
    lax.fori_loop(0, n_chunks, scan_body, (zero, zero))
    y = jnp.dot(xb, mt_ref[...], preferred_element_type=F32)
    y = y + jnp.dot(sr_ref[...].astype(BF16), qre_ref[0], preferred_element_type=F32)
    y = y + jnp.dot(si_ref[...].astype(BF16), qim_ref[0], preferred_element_type=F32)
    y_ref[0] = y.astype(y_ref.dtype)


def _s5(u, a_re, a_im, log_dt, b_re, b_im, c_re, c_im):
    l = u.shape[0]
    t, c, p, g = SSM_T, SSM_GROUP_CH, SSM_STATE, SSM_GROUPS
    n_chunks = l // t
    width = t * c
    dt = jnp.exp(log_dt.astype(F32))[:, None]
    ar, ai = a_re.astype(F32), a_im.astype(F32)
    mag = jnp.exp(ar * dt)
    lb_re, lb_im = mag * jnp.cos(ai * dt), mag * jnp.sin(ai * dt)
    den = ar * ar + ai * ai
    coef_re = ((lb_re - 1.0) * ar + lb_im * ai) / den
    coef_im = (lb_im * ar - (lb_re - 1.0) * ai) / den
    br, bi = b_re.astype(F32), b_im.astype(F32)
    bb_re = coef_re[..., None] * br - coef_im[..., None] * bi
    bb_im = coef_re[..., None] * bi + coef_im[..., None] * br
    cr, ci = c_re.astype(F32), c_im.astype(F32)
    pr, pi = jnp.ones((g, 1, p), F32), jnp.zeros((g, 1, p), F32)
    sr, si = lb_re[:, None, :], lb_im[:, None, :]
    while pr.shape[1] < t + 1:
        pr, pi = (jnp.concatenate([pr, pr * sr - pi * si], axis=1),
                  jnp.concatenate([pi, pr * si + pi * sr], axis=1))
        sr, si = sr * sr - si * si, 2.0 * sr * si
    pr, pi = pr[:, :t + 1], pi[:, :t + 1]
    cr_t, ci_t = cr.transpose(0, 2, 1)[:, :, None, :], ci.transpose(0, 2, 1)[:, :, None, :]
    pr_t, pi_t = pr.transpose(0, 2, 1)[:, :, :, None], pi.transpose(0, 2, 1)[:, :, :, None]
    gc_re = cr_t * pr_t - ci_t * pi_t
    gc_im = cr_t * pi_t + ci_t * pr_t
    tb_re, tb_im = bb_re.transpose(0, 2, 1), bb_im.transpose(0, 2, 1)
    bcat = jnp.concatenate([tb_re, -tb_im], axis=2)
    gcat = jnp.concatenate([gc_re[:, :, :t], gc_im[:, :, :t]], axis=1).reshape(g, 2 * p, width)
    rr, ri = pr[:, :t][:, ::-1][:, :, None, :], pi[:, :t][:, ::-1][:, :, None, :]
    pe_re = (rr * tb_re[:, None] - ri * tb_im[:, None]).reshape(g, width, p)
    pe_im = (rr * tb_im[:, None] + ri * tb_re[:, None]).reshape(g, width, p)
    q_re = gc_re[:, :, 1:t + 1].reshape(g, p, width)
    q_im = gc_im[:, :, 1:t + 1].reshape(g, p, width)
    lam_pack = jnp.stack([pr[:, t], pi[:, t]], axis=1)
    x = u.astype(BF16).reshape(n_chunks, t, g, c).transpose(2, 0, 1, 3).reshape(g, n_chunks, width)
    grp3 = lambda a, b: pl.BlockSpec((1, a, b), lambda i: (i, 0, 0))
    y = pl.pallas_call(
        _s5_kernel,
        grid=(g,),
        in_specs=[grp3(n_chunks, width), grp3(c, 2 * p), grp3(2 * p, width), grp3(width, p), grp3(width, p),
                  grp3(p, width), grp3(p, width), grp3(2, p)],
        out_specs=grp3(n_chunks, width),
        out_shape=jax.ShapeDtypeStruct((g, n_chunks, width), BF16),
        scratch_shapes=[
            pltpu.VMEM((width, width), BF16),
            pltpu.VMEM((n_chunks, p), F32),
            pltpu.VMEM((n_chunks, p), F32),
            pltpu.VMEM((n_chunks, p), F32),
            pltpu.VMEM((n_chunks, p), F32),
        ],
        compiler_params=_params(("parallel",)),
        name="s5",
    )(x, bcat, gcat, pe_re.astype(BF16), pe_im.astype(BF16), q_re.astype(BF16), (-q_im).astype(BF16), lam_pack)
    return y.reshape(g, n_chunks, t, c).transpose(1, 2, 0, 3).reshape(l, SSM_WIDTH)


def _gelu_tanh(x):
    return 0.5 * x * (1.0 + jnp.tanh(math.sqrt(2.0 / math.pi) * (x + 0.044715 * (x * x * x))))


def _out_kernel(att_ref, ag_ref, ys_ref, u_ref, sg_ref, gate_ref, x_ref, d_ref, watt_ref, wglu_ref, bglu_ref,
                wssm_ref, wout_ref, gfin_ref, o_ref):
    a = (att_ref[...].astype(F32) * ag_ref[...].astype(F32)).astype(BF16)
    y_a = jnp.dot(a, watt_ref[...], preferred_element_type=F32)
    s = _gelu_tanh(ys_ref[...].astype(F32) + d_ref[...] * u_ref[...])
    z = jnp.dot(s.astype(BF16), wglu_ref[...], preferred_element_type=F32) + bglu_ref[...]
    s = s * jax.nn.sigmoid(z)
    y_b = jnp.dot((s * sg_ref[...]).astype(BF16), wssm_ref[...], preferred_element_type=F32)
    gates = gate_ref[...].astype(F32)
    mixed = gates[:, :D_MODEL] * y_a + gates[:, D_MODEL:] * y_b
    r = x_ref[...] + jnp.dot(mixed.astype(BF16), wout_ref[...], preferred_element_type=F32)
    y = r * lax.rsqrt(jnp.mean(r * r, axis=-1, keepdims=True) + EPS)
    o_ref[...] = (y * gfin_ref[...]).astype(o_ref.dtype)


def _out_stage(att, att_gate, y_ssm, u_gate, gates, x, d_skip, w_att_out, w_glu, b_glu, w_ssm_out, w_out,
               g_final, tm=256):
    l = x.shape[0]
    row = lambda n: pl.BlockSpec((tm, n), lambda i: (i, 0))
    return pl.pallas_call(
        _out_kernel,
        grid=(l // tm,),
        in_specs=[row(ATT_WIDTH), row(ATT_WIDTH), row(SSM_WIDTH), row(SSM_WIDTH),
                  pl.BlockSpec((tm, SSM_WIDTH), lambda i: (i, 1)),
                  row(N_BRANCH * D_MODEL), row(D_MODEL), _resident((1, SSM_WIDTH)),
                  _resident((ATT_WIDTH, D_MODEL)), _resident((SSM_WIDTH, SSM_WIDTH)), _resident((1, SSM_WIDTH)),
                  _resident((SSM_WIDTH, D_MODEL)), _resident((D_MODEL, D_MODEL)), _resident((1, D_MODEL))],
        out_specs=row(D_MODEL),
        out_shape=jax.ShapeDtypeStruct((l, D_MODEL), x.dtype),
        compiler_params=_params(("parallel",)),
        name="out_stage",
    )(att, att_gate, y_ssm, u_gate, u_gate, gates, x, d_skip.astype(F32).reshape(1, -1), w_att_out.astype(BF16),
      w_glu.astype(BF16), b_glu.reshape(1, -1), w_ssm_out.astype(BF16), w_out.astype(BF16), g_final.reshape(1, -1))


def _layer(x, positions, g_norm, w_in, b_merge, a_re, a_im, log_dt, ssm_b_re, ssm_b_im, ssm_c_re, ssm_c_im,
           ssm_d, w_glu, b_glu, w_att_out, w_ssm_out, w_out, g_out):
    off = [0]
    for s in SPLITS:
        off.append(off[-1] + s)
    head_end = off[7] + LANES - IDX_DIM - IDX_HEADS
    w_qkv = w_in[:, :off[3]].astype(BF16)
    mid0 = off[3] - ATT_WIDTH
    w_mid = w_in[:, mid0:head_end].astype(BF16)
    w_tail = w_in[:, off[7]:].astype(BF16)
    rope_att = _rope_tables(positions, HEAD_DIM)
    rope_idx = _rope_tables(positions, IDX_DIM)

    q_scale = HEAD_DIM ** -0.5 * LOG2E
    q_t, h = _project("q", x, w_qkv, off[0], ATT_WIDTH, ((0, ATT_WIDTH, "rope128", q_scale),), BF16, rope_att,
                      transpose_out=True, norm_gain=g_norm)
    kv_t = _project("kv", h, w_qkv, off[1], 2 * KV_WIDTH,
                    ((0, KV_WIDTH, "rope128", 1.0), (KV_WIDTH, KV_WIDTH, "none", 1.0)), BF16, rope_att,
                    transpose_out=True)
    att_gate = _project("att_gate", h, w_mid, off[3] - mid0, ATT_WIDTH, ((0, ATT_WIDTH, "silu", 1.0),), BF16)
    q_idx_t = _project("q_idx", h, w_mid, off[4] - mid0, IDX_HEADS * IDX_DIM,
                       ((0, IDX_HEADS * IDX_DIM, "rope64", 1.0),), BF16, rope_idx, transpose_out=True)
    misc_t = _project("idx_misc", h, w_mid, off[5] - mid0, LANES,
                      ((0, LANES, "idx_misc", IDX_HEADS ** -0.5 * IDX_DIM ** -0.5),), F32, rope_idx,
                      transpose_out=True)
    u_gate = _project("ssm", h, w_tail, 0, 2 * SSM_WIDTH,
                      ((0, SSM_WIDTH, "none", 1.0), (SSM_WIDTH, SSM_WIDTH, "silu", 1.0)), F32)
    gates = _project("merge", h, w_tail, 2 * SSM_WIDTH, N_BRANCH * D_MODEL, ((0, 1024, "sigmoid", 1.0),), BF16,
                     bias=b_merge, tm=1024, tn=1024)

    k_idx = misc_t[:IDX_DIM].T.astype(BF16)
    w_idx_t = misc_t[IDX_DIM:IDX_DIM + IDX_HEADS]
    att = _dsa(q_t, kv_t[:KV_WIDTH], kv_t[KV_WIDTH:], q_idx_t, k_idx, w_idx_t, BF16)
    y_ssm = _s5(u_gate[:, :SSM_WIDTH], a_re, a_im, log_dt, ssm_b_re, ssm_b_im, ssm_c_re, ssm_c_im)
    return _out_stage(att, att_gate, y_ssm, u_gate, gates, x, ssm_d, w_att_out, w_glu, b_glu, w_ssm_out, w_out, g_out)


def kernel(x, positions, g_norm, w_in, b_merge, a_re, a_im, log_dt, ssm_b_re, ssm_b_im, ssm_c_re, ssm_c_im,
           ssm_d, w_glu, b_glu, w_att_out, w_ssm_out, w_out, g_final):
    depth = w_in.shape[0]
    assert depth == 1 and x.shape[0] == 1, "single layer, single sequence"
    y = _layer(x[0], positions[0], g_norm[0], w_in[0], b_merge[0], a_re[0], a_im[0], log_dt[0], ssm_b_re[0],
               ssm_b_im[0], ssm_c_re[0], ssm_c_im[0], ssm_d[0], w_glu[0], b_glu[0], w_att_out[0], w_ssm_out[0],
               w_out[0], g_final)
    return y[None]
```

```python
import functools
import math

import jax
import jax.numpy as jnp
from jax import lax
from jax.experimental import pallas as pl
from jax.experimental.pallas import tpu as pltpu

D_MODEL = 2048
CHUNK = 64
EPS = 1e-6
ATT_HEADS = 8
ATT_KV_HEADS = 2
HEAD_DIM = 128
ATT_WIDTH = ATT_HEADS * HEAD_DIM
KV_WIDTH = ATT_KV_HEADS * HEAD_DIM
IDX_HEADS = 16
IDX_DIM = 64
INDEX_TOPK = 256
ROPE_THETA = 500000.0
ROPE_FRACTION = 4
SSM_GROUP_CH = 16
SSM_GROUPS = 32
SSM_WIDTH = SSM_GROUPS * SSM_GROUP_CH
SSM_STATE = 64
N_BRANCH = 2
SPLITS = (ATT_WIDTH, KV_WIDTH, KV_WIDTH, ATT_WIDTH, IDX_HEADS * IDX_DIM, IDX_DIM, IDX_HEADS,
          SSM_WIDTH, SSM_WIDTH, N_BRANCH * D_MODEL)

LANES = 128
ROPE_FREQS = HEAD_DIM // ROPE_FRACTION // 2
VMEM_LIMIT = 56 * 1024 * 1024

Q_TILE = 256
KEY_TILE = 256
ATT_CHUNK = 512
V_ROWS = HEAD_DIM + 16
BLIND_PASSES = 18
SSM_T = 64
LOG2E = 1.4426950408889634
NEG_BIG = -1e30

F32 = jnp.float32
BF16 = jnp.bfloat16


def _params(semantics):
    return pltpu.CompilerParams(dimension_semantics=semantics, vmem_limit_bytes=VMEM_LIMIT)


def _resident(shape):
    zeros = (0,) * len(shape)
    return pl.BlockSpec(shape, lambda *_: zeros, pipeline_mode=pl.Buffered(1))


def _rope_kernel(pos_ref, freq_ref, cos_ref, sin_ref):
    ang = freq_ref[...] * pos_ref[...].astype(F32)
    cos_ref[...] = jnp.cos(ang)
    sin_ref[...] = jnp.sin(ang)


def _rope_tables(positions, head_dim):
    l = positions.shape[0]
    rd = head_dim // ROPE_FRACTION
    half = rd // 2
    inv_freq = jnp.exp(-math.log(ROPE_THETA) * jnp.arange(half, dtype=F32) * (2.0 / rd))
    return pl.pallas_call(
        _rope_kernel,
        out_shape=(jax.ShapeDtypeStruct((half, l), F32),) * 2,
        name="rope_tables_%d" % head_dim,
    )(positions.reshape(1, l), inv_freq.reshape(half, 1))


def _silu(x):
    return x * jax.nn.sigmoid(x)


def _rotate_rows(xt, cos, sin, head_dim):
    half = head_dim // ROPE_FRACTION // 2
    pieces = []
    for base in range(0, LANES, head_dim):
        top, bot = xt[base:base + half], xt[base + half:base + 2 * half]
        pieces += [top * cos - bot * sin, bot * cos + top * sin, xt[base + 2 * half:base + head_dim]]
    return jnp.concatenate(pieces, axis=0)


def _proj_kernel(*refs, segments, has_rope, has_bias, has_norm, transpose_out):
    h_ref, w_ref = refs[0], refs[1]
    pos = 2
    if has_norm:
        gain_ref = refs[pos]
        pos += 1
    if has_bias:
        bias_ref = refs[pos]
        pos += 1
    if has_rope:
        cos, sin = refs[pos][...], refs[pos + 1][...]
        pos += 2
    o_ref = refs[pos]
    if has_norm:
        x = h_ref[...]
        h = (x * lax.rsqrt(jnp.mean(x * x, axis=-1, keepdims=True) + EPS) * gain_ref[...]).astype(BF16)
        refs[pos + 1][...] = h
    else:
        h = h_ref[...]
    acc = jnp.dot(h, w_ref[...], preferred_element_type=F32)
    if has_bias:
        acc = acc + bias_ref[...]
    for start, size, kind, scale in segments:
        for b in range(size // LANES):
            lo = start + b * LANES
            xb = acc[:, lo:lo + LANES]
            if kind == "silu":
                xb = _silu(xb)
            elif kind == "sigmoid":
                xb = jax.nn.sigmoid(xb)
            if not transpose_out:
                assert kind in ("none", "silu", "sigmoid") and scale == 1.0
                o_ref[:, lo:lo + LANES] = xb.astype(o_ref.dtype)
                continue
            xt = xb.T
            if kind == "rope128":
                xt = _rotate_rows(xt, cos, sin, HEAD_DIM)
            elif kind == "rope64":
                xt = _rotate_rows(xt, cos, sin, IDX_DIM)
            elif kind == "idx_misc":
                xt = jnp.concatenate([_rotate_rows(xt, cos, sin, IDX_DIM)[:IDX_DIM], xt[IDX_DIM:] * scale], axis=0)
            if kind != "idx_misc" and scale != 1.0:
                xt = xt * scale
            o_ref[lo:lo + LANES, :] = xt.astype(o_ref.dtype)


def _project(name, h, w, col0, n, segments, out_dtype, rope=None, bias=None, tm=512, tn=None,
             transpose_out=False, norm_gain=None):
    l, d = h.shape
    tn = n if tn is None else tn
    assert col0 % tn == 0 and n % tn == 0
    assert norm_gain is None or n == tn
    jb = col0 // tn
    if transpose_out:
        out_spec = pl.BlockSpec((tn, tm), lambda i, j: (j, i))
        out_shape = jax.ShapeDtypeStruct((n, l), out_dtype)
    else:
        out_spec = pl.BlockSpec((tm, tn), lambda i, j: (i, j))
        out_shape = jax.ShapeDtypeStruct((l, n), out_dtype)
    in_specs = [pl.BlockSpec((tm, d), lambda i, j: (i, 0)), pl.BlockSpec((d, tn), lambda i, j: (0, jb + j))]
    args = [h, w]
    if norm_gain is not None:
        in_specs.append(pl.BlockSpec((1, d), lambda i, j: (0, 0)))
        args.append(norm_gain.reshape(1, d))
        out_spec = (out_spec, pl.BlockSpec((tm, d), lambda i, j: (i, 0)))
        out_shape = (out_shape, jax.ShapeDtypeStruct((l, d), BF16))
    if bias is not None:
        in_specs.append(pl.BlockSpec((1, tn), lambda i, j: (0, j)))
        args.append(bias.reshape(1, n))
    if rope is not None:
        assert transpose_out
        in_specs += [pl.BlockSpec((rope[0].shape[0], tm), lambda i, j: (0, i))] * 2
        args += list(rope)
    kern = functools.partial(_proj_kernel, segments=segments, has_rope=rope is not None, has_bias=bias is not None,
                             has_norm=norm_gain is not None, transpose_out=transpose_out)
    return pl.pallas_call(
        kern,
        grid=(l // tm, n // tn),
        in_specs=in_specs,
        out_specs=out_spec,
        out_shape=out_shape,
        compiler_params=_params(("parallel", "arbitrary")),
        name="in_proj_" + name,
    )(*args)


def _col_reduce(x, op):
    r, q = x.shape
    y = op(x.reshape(r // 64, 64, q), axis=0) if r > 64 else x
    y = op(y.reshape(8, 8, q), axis=0)
    return op(y, axis=0, keepdims=True)


def _dsa_kernel(qt_ref, qit_ref, wt_ref, kidx_ref, k_ref, vt_ref, o_ref,
                sc_ref, s_ref, acc_ref, *, topk, n_att_chunks):
    i = pl.program_id(0)
    n_blk = i + 1
    sub = 8
    grp = ATT_HEADS // ATT_KV_HEADS
    inf = float("inf")

    def score_block(t, carry, diagonal):
        smin, smax = carry
        rows = pl.ds(pl.multiple_of(t * KEY_TILE, KEY_TILE), KEY_TILE)
        kt = kidx_ref[rows, :]
        acc = jnp.zeros((KEY_TILE, Q_TILE), F32)
        for h in range(IDX_HEADS):
            logit = jnp.dot(kt, qit_ref[h * IDX_DIM:(h + 1) * IDX_DIM, :], preferred_element_type=F32)
            acc = acc + wt_ref[h:h + 1, :] * jnp.maximum(logit, 0.0)
        lo_src = acc
        if diagonal:
            r_chunk = lax.broadcasted_iota(jnp.int32, acc.shape, 0) // CHUNK
            q_chunk = lax.broadcasted_iota(jnp.int32, acc.shape, 1) // CHUNK
            lo_src = jnp.where(r_chunk <= q_chunk, acc, inf)
            acc = jnp.where(r_chunk <= q_chunk, acc, -inf)
        sc_ref[rows, :] = acc
        return (jnp.minimum(smin, jnp.min(lo_src.reshape(KEY_TILE // sub, sub, Q_TILE), axis=0)),
                jnp.maximum(smax, jnp.max(acc.reshape(KEY_TILE // sub, sub, Q_TILE), axis=0)))

    carry = lax.fori_loop(0, i, lambda t, c: score_block(t, c, False),
                          (jnp.full((sub, Q_TILE), inf, F32), jnp.full((sub, Q_TILE), -inf, F32)))
    smin, smax = score_block(i, carry, True)
    smin = jnp.min(smin, axis=0, keepdims=True)
    smax = jnp.max(smax, axis=0, keepdims=True)

    blk_per_it = 2 * ATT_CHUNK // KEY_TILE
    n_it = (n_blk + blk_per_it - 1) // blk_per_it

    def fill_body(t, carry):
        sc_ref[pl.ds(pl.multiple_of(t * KEY_TILE, KEY_TILE), KEY_TILE), :] = jnp.full(
            (KEY_TILE, Q_TILE), -inf, F32)
        return carry

    lax.fori_loop(n_blk, n_it * blk_per_it, fill_body, 0)

    def count_ge(cand):
        def count_rows(start, rows, accs):
            blk = sc_ref[pl.ds(pl.multiple_of(start, KEY_TILE), rows), :]
            hit = (blk >= cand).astype(jnp.int32)
            return accs + jnp.sum(hit.reshape(4, rows // (4 * sub), sub, Q_TILE), axis=1)

        accs = lax.fori_loop(0, n_blk // 2, lambda t, a: count_rows(t * 2 * KEY_TILE, 2 * KEY_TILE, a),
                             jnp.zeros((4, sub, Q_TILE), jnp.int32))
        accs = lax.fori_loop(2 * (n_blk // 2), n_blk, lambda t, a: count_rows(t * KEY_TILE, KEY_TILE, a), accs)
        return jnp.sum(jnp.sum(accs, axis=0), axis=0, keepdims=True)

    lane_chunk = lax.broadcasted_iota(jnp.int32, (1, Q_TILE), 1) // CHUNK
    n_adm = (i * (Q_TILE // CHUNK) + lane_chunk + 1) * CHUNK
    hi0 = smax + (jnp.abs(smax) * 1e-6 + 1e-30)

    def bis_step(state):
        lo, hi, cnt_lo, cnt_hi, live_i = state
        mid = lo + 0.5 * (hi - lo)
        live = (live_i > 0) & (mid > lo) & (mid < hi)
        cnt = count_ge(mid)
        ge = cnt >= topk
        lo = jnp.where(live & ge, mid, lo)
        hi = jnp.where(live & ~ge, mid, hi)
        cnt_lo = jnp.where(live & ge, cnt, cnt_lo)
        cnt_hi = jnp.where(live & ~ge, cnt, cnt_hi)
        return lo, hi, cnt_lo, cnt_hi, (live & (cnt_lo > topk)).astype(jnp.int32)

    def bis_body(carry):
        state = bis_step(carry[1:])
        return (jnp.max(state[4]),) + state

    state = lax.fori_loop(0, BLIND_PASSES, lambda _, st: bis_step(st),
                          (smin, hi0, n_adm, jnp.zeros((1, Q_TILE), jnp.int32), (n_adm > topk).astype(jnp.int32)))
    _, thr, _, cnt_lo, cnt_hi, _ = lax.while_loop(lambda c: c[0] > 0, bis_body, (jnp.max(state[4]),) + state)

    tied = cnt_lo > topk

    @pl.when(jnp.max(tied.astype(jnp.int32)) > 0)
    def _():
        keep = (topk - cnt_hi).astype(F32)
        tri = (lax.broadcasted_iota(jnp.int32, (KEY_TILE, KEY_TILE), 0)
               >= lax.broadcasted_iota(jnp.int32, (KEY_TILE, KEY_TILE), 1)).astype(BF16)

        def fix_body(t, before):
            rows = pl.ds(pl.multiple_of(t * KEY_TILE, KEY_TILE), KEY_TILE)
            blk = sc_ref[rows, :]
            eq = (blk == thr) & tied
            eq_f = eq.astype(F32)
            rank = jnp.dot(tri, eq_f.astype(BF16), preferred_element_type=F32) - 1.0 + before
            sc_ref[rows, :] = jnp.where(eq & (rank >= keep), -inf, blk)
            return before + jnp.sum(eq_f, axis=0, keepdims=True)

        lax.fori_loop(0, n_blk, fix_body, jnp.zeros((1, Q_TILE), F32))

    last_chunk = n_att_chunks - 1

    def qk(h, c):
        rows = pl.ds(pl.multiple_of(c * ATT_CHUNK, ATT_CHUNK), ATT_CHUNK)
        return jnp.dot(k_ref[h // grp, rows, :], qt_ref[h * HEAD_DIM:(h + 1) * HEAD_DIM, :],
                       preferred_element_type=F32)

    for kv in range(ATT_KV_HEADS):
        heads = range(kv * grp, (kv + 1) * grp)
        acc_ref[...] = jnp.zeros(acc_ref.shape, F32)
        for g, h in enumerate(heads):
            s_ref[0, g] = qk(h, 0)

        def att_body(it, carry, kv=kv, heads=heads):
            ms, ls = list(carry[0]), list(carry[1])
            for si in range(2):
                c = it * 2 + si
                rows = pl.ds(pl.multiple_of(c * ATT_CHUNK, ATT_CHUNK), ATT_CHUNK)
                bias = jnp.where(sc_ref[rows, :] >= thr, 0.0, NEG_BIG)
                v_t = vt_ref[c, kv * V_ROWS:(kv + 1) * V_ROWS, :]
                for g, h in enumerate(heads):
                    s = s_ref[si, g] + bias
                    s_ref[1 - si, g] = qk(h, jnp.minimum(c + 1, last_chunk))
                    m_new = jnp.maximum(ms[g], _col_reduce(s, jnp.max))
                    alpha = jnp.exp2(ms[g] - m_new)
                    p = jnp.exp2(s - m_new)
                    pv = jnp.dot(v_t, p.astype(BF16), preferred_element_type=F32)
                    ls[g] = alpha * ls[g] + pv[HEAD_DIM:HEAD_DIM + 1]
                    acc_ref[g] = alpha * acc_ref[g] + pv[:HEAD_DIM]
                    ms[g] = m_new
            return tuple(ms), tuple(ls)

        m0 = tuple(jnp.full((1, Q_TILE), NEG_BIG, F32) for _ in heads)
        l0 = tuple(jnp.zeros((1, Q_TILE), F32) for _ in heads)
        _, ls = lax.fori_loop(0, n_it, att_body, (m0, l0))
        for g, h in enumerate(heads):
            o_ref[:, h * HEAD_DIM:(h + 1) * HEAD_DIM] = (acc_ref[g] / ls[g]).T.astype(o_ref.dtype)


def _dsa(q_t, k_t, v_t, q_idx_t, k_idx, w_idx_t, out_dtype):
    l = q_t.shape[1]
    topk = min(INDEX_TOPK, l // 4)
    assert l % (2 * ATT_CHUNK) == 0
    n_att_chunks = l // ATT_CHUNK
    k3 = k_t.reshape(ATT_KV_HEADS, HEAD_DIM, l).transpose(0, 2, 1)
    vt = v_t.reshape(ATT_KV_HEADS, HEAD_DIM, n_att_chunks, ATT_CHUNK)
    vt = jnp.concatenate([vt, jnp.ones((ATT_KV_HEADS, V_ROWS - HEAD_DIM, n_att_chunks, ATT_CHUNK), vt.dtype)], axis=1)
    vt = vt.reshape(ATT_KV_HEADS * V_ROWS, n_att_chunks, ATT_CHUNK).transpose(1, 0, 2)
    kern = functools.partial(_dsa_kernel, topk=topk, n_att_chunks=n_att_chunks)
    return pl.pallas_call(
        kern,
        grid=(l // Q_TILE,),
        in_specs=[
            pl.BlockSpec((ATT_WIDTH, Q_TILE), lambda i: (0, i)),
            pl.BlockSpec((IDX_HEADS * IDX_DIM, Q_TILE), lambda i: (0, i)),
            pl.BlockSpec((IDX_HEADS, Q_TILE), lambda i: (0, i)),
            _resident((l, IDX_DIM)),
            _resident((ATT_KV_HEADS, l, HEAD_DIM)),
            _resident((n_att_chunks, ATT_KV_HEADS * V_ROWS, ATT_CHUNK)),
        ],
        out_specs=pl.BlockSpec((Q_TILE, ATT_WIDTH), lambda i: (i, 0)),
        out_shape=jax.ShapeDtypeStruct((l, ATT_WIDTH), out_dtype),
        scratch_shapes=[
            pltpu.VMEM((l, Q_TILE), F32),
            pltpu.VMEM((2, ATT_HEADS // ATT_KV_HEADS, ATT_CHUNK, Q_TILE), F32),
            pltpu.VMEM((ATT_HEADS // ATT_KV_HEADS, HEAD_DIM, Q_TILE), F32),
        ],
        compiler_params=_params(("arbitrary",)),
        name="dsa",
    )(q_t, q_idx_t, w_idx_t, k_idx, k3, vt)


def _s5_kernel(x_ref, bcat_ref, gcat_ref, per_ref, pei_ref, qre_ref, qim_ref, lam_ref, y_ref,
               mt_ref, er_ref, ei_ref, sr_ref, si_ref):
    r0 = jnp.dot(bcat_ref[0], gcat_ref[0], precision=lax.Precision.HIGHEST, preferred_element_type=F32)
    lane = lax.broadcasted_iota(jnp.int32, r0.shape, 1)
    for s in range(SSM_T):
        shifted = r0 if s == 0 else pltpu.roll(r0, s * SSM_GROUP_CH, 1)
        mt_ref[s * SSM_GROUP_CH:(s + 1) * SSM_GROUP_CH, :] = jnp.where(
            lane >= s * SSM_GROUP_CH, shifted, 0.0).astype(BF16)
    xb = x_ref[0]
    er_ref[...] = jnp.dot(xb, per_ref[0], preferred_element_type=F32)
    ei_ref[...] = jnp.dot(xb, pei_ref[0], preferred_element_type=F32)
    lam_r = lam_ref[0, 0:1, :]
    lam_i = lam_ref[0, 1:2, :]
    n_chunks = xb.shape[0]

    def scan_body(c, carry):
        s_r, s_i = carry
        sr_ref[pl.ds(c, 1), :] = s_r
        si_ref[pl.ds(c, 1), :] = s_i
        e_r = er_ref[pl.ds(c, 1), :]
        e_i = ei_ref[pl.ds(c, 1), :]
        return (lam_r * s_r - lam_i * s_i + e_r, lam_r * s_i + lam_i * s_r + e_i)

    zero = jnp.zeros((1, SSM_STATE), F32)
    lax.fori_loop(0, n_chunks, scan_body, (zero, zero))
    y = jnp.dot(xb, mt_ref[...], preferred_element_type=F32)
    y = y + jnp.dot(sr_ref[...].astype(BF16), qre_ref[0], preferred_element_type=F32)
    y = y + jnp.dot(si_ref[...].astype(BF16), qim_ref[0], preferred_element_type=F32)
    y_ref[0] = y.astype(y_ref.dtype)


def _s5(u, a_re, a_im, log_dt, b_re, b_im, c_re, c_im):
    l = u.shape[0]
    t, c, p, g = SSM_T, SSM_GROUP_CH, SSM_STATE, SSM_GROUPS
    n_chunks = l // t
    width = t * c
    dt = jnp.exp(log_dt.astype(F32))[:, None]
    ar, ai = a_re.astype(F32), a_im.astype(F32)
    mag = jnp.exp(ar * dt)
    lb_re, lb_im = mag * jnp.cos(ai * dt), mag * jnp.sin(ai * dt)
    den = ar * ar + ai * ai
    coef_re = ((lb_re - 1.0) * ar + lb_im * ai) / den
    coef_im = (lb_im * ar - (lb_re - 1.0) * ai) / den
    br, bi = b_re.astype(F32), b_im.astype(F32)
    bb_re = coef_re[..., None] * br - coef_im[..., None] * bi
    bb_im = coef_re[..., None] * bi + coef_im[..., None] * br
    cr, ci = c_re.astype(F32), c_im.astype(F32)
    pr, pi = jnp.ones((g, 1, p), F32), jnp.zeros((g, 1, p), F32)
    sr, si = lb_re[:, None, :], lb_im[:, None, :]
    while pr.shape[1] < t + 1:
        pr, pi = (jnp.concatenate([pr, pr * sr - pi * si], axis=1),
                  jnp.concatenate([pi, pr * si + pi * sr], axis=1))
        sr, si = sr * sr - si * si, 2.0 * sr * si
    pr, pi = pr[:, :t + 1], pi[:, :t + 1]
    cr_t, ci_t = cr.transpose(0, 2, 1)[:, :, None, :], ci.transpose(0, 2, 1)[:, :, None, :]
    pr_t, pi_t = pr.transpose(0, 2, 1)[:, :, :, None], pi.transpose(0, 2, 1)[:, :, :, None]
    gc_re = cr_t * pr_t - ci_t * pi_t
    gc_im = cr_t * pi_t + ci_t * pr_t
    tb_re, tb_im = bb_re.transpose(0, 2, 1), bb_im.transpose(0, 2, 1)
    bcat = jnp.concatenate([tb_re, -tb_im], axis=2)
    gcat = jnp.concatenate([gc_re[:, :, :t], gc_im[:, :, :t]], axis=1).reshape(g, 2 * p, width)
    rr, ri = pr[:, :t][:, ::-1][:, :, None, :], pi[:, :t][:, ::-1][:, :, None, :]
    pe_re = (rr * tb_re[:, None] - ri * tb_im[:, None]).reshape(g, width, p)
    pe_im = (rr * tb_im[:, None] + ri * tb_re[:, None]).reshape(g, width, p)
    q_re = gc_re[:, :, 1:t + 1].reshape(g, p, width)
    q_im = gc_im[:, :, 1:t + 1].reshape(g, p, width)
    lam_pack = jnp.stack([pr[:, t], pi[:, t]], axis=1)
    x = u.astype(BF16).reshape(n_chunks, t, g, c).transpose(2, 0, 1, 3).reshape(g, n_chunks, width)
    grp3 = lambda a, b: pl.BlockSpec((1, a, b), lambda i: (i, 0, 0))
    y = pl.pallas_call(
        _s5_kernel,
        grid=(g,),
        in_specs=[grp3(n_chunks, width), grp3(c, 2 * p), grp3(2 * p, width), grp3(width, p), grp3(width, p),
                  grp3(p, width), grp3(p, width), grp3(2, p)],
        out_specs=grp3(n_chunks, width),
        out_shape=jax.ShapeDtypeStruct((g, n_chunks, width), BF16),
        scratch_shapes=[
            pltpu.VMEM((width, width), BF16),
            pltpu.VMEM((n_chunks, p), F32),
            pltpu.VMEM((n_chunks, p), F32),
            pltpu.VMEM((n_chunks, p), F32),
            pltpu.VMEM((n_chunks, p), F32),
        ],
        compiler_params=_params(("parallel",)),
        name="s5",
    )(x, bcat, gcat, pe_re.astype(BF16), pe_im.astype(BF16), q_re.astype(BF16), (-q_im).astype(BF16), lam_pack)
    return y.reshape(g, n_chunks, t, c).transpose(1, 2, 0, 3).reshape(l, SSM_WIDTH)


def _gelu_tanh(x):
    return 0.5 * x * (1.0 + jnp.tanh(math.sqrt(2.0 / math.pi) * (x + 0.044715 * (x * x * x))))


def _out_kernel(att_ref, ag_ref, ys_ref, u_ref, sg_ref, gate_ref, x_ref, d_ref, watt_ref, wglu_ref, bglu_ref,
                wssm_ref, wout_ref, gfin_ref, o_ref):
    a = (att_ref[...].astype(F32) * ag_ref[...].astype(F32)).astype(BF16)
    y_a = jnp.dot(a, watt_ref[...], preferred_element_type=F32)
    s = _gelu_tanh(ys_ref[...].astype(F32) + d_ref[...] * u_ref[...])
    z = jnp.dot(s.astype(BF16), wglu_ref[...], preferred_element_type=F32) + bglu_ref[...]
    s = s * jax.nn.sigmoid(z)
    y_b = jnp.dot((s * sg_ref[...]).astype(BF16), wssm_ref[...], preferred_element_type=F32)
    gates = gate_ref[...].astype(F32)
    mixed = gates[:, :D_MODEL] * y_a + gates[:, D_MODEL:] * y_b
    r = x_ref[...] + jnp.dot(mixed.astype(BF16), wout_ref[...], preferred_element_type=F32)
    y = r * lax.rsqrt(jnp.mean(r * r, axis=-1, keepdims=True) + EPS)
    o_ref[...] = (y * gfin_ref[...]).astype(o_ref.dtype)


def _out_stage(att, att_gate, y_ssm, u_gate, gates, x, d_skip, w_att_out, w_glu, b_glu, w_ssm_out, w_out,
               g_final, tm=256):
    l = x.shape[0]
    row = lambda n: pl.BlockSpec((tm, n), lambda i: (i, 0))
    return pl.pallas_call(
        _out_kernel,
        grid=(l // tm,),
        in_specs=[row(ATT_WIDTH), row(ATT_WIDTH), row(SSM_WIDTH), row(SSM_WIDTH),
                  pl.BlockSpec((tm, SSM_WIDTH), lambda i: (i, 1)),
                  row(N_BRANCH * D_MODEL), row(D_MODEL), _resident((1, SSM_WIDTH)),
                  _resident((ATT_WIDTH, D_MODEL)), _resident((SSM_WIDTH, SSM_WIDTH)), _resident((1, SSM_WIDTH)),
                  _resident((SSM_WIDTH, D_MODEL)), _resident((D_MODEL, D_MODEL)), _resident((1, D_MODEL))],
        out_specs=row(D_MODEL),
        out_shape=jax.ShapeDtypeStruct((l, D_MODEL), x.dtype),
        compiler_params=_params(("parallel",)),
        name="out_stage",
    )(att, att_gate, y_ssm, u_gate, u_gate, gates, x, d_skip.astype(F32).reshape(1, -1), w_att_out.astype(BF16),
      w_glu.astype(BF16), b_glu.reshape(1, -1), w_ssm_out.astype(BF16), w_out.astype(BF16), g_final.reshape(1, -1))


def _layer(x, positions, g_norm, w_in, b_merge, a_re, a_im, log_dt, ssm_b_re, ssm_b_im, ssm_c_re, ssm_c_im,
           ssm_d, w_glu, b_glu, w_att_out, w_ssm_out, w_out, g_out):
    off = [0]
    for s in SPLITS:
        off.append(off[-1] + s)
    head_end = off[7] + LANES - IDX_DIM - IDX_HEADS
    w_qkv = w_in[:, :off[3]].astype(BF16)
    mid0 = off[3] - ATT_WIDTH
    w_mid = w_in[:, mid0:head_end].astype(BF16)
    w_tail = w_in[:, off[7]:].astype(BF16)
    rope_att = _rope_tables(positions, HEAD_DIM)
    rope_idx = _rope_tables(positions, IDX_DIM)

    q_scale = HEAD_DIM ** -0.5 * LOG2E
    qkv_t, h = _project("qkv", x, w_qkv, 0, off[3],
                        ((off[0], ATT_WIDTH, "rope128", q_scale), (off[1], KV_WIDTH, "rope128", 1.0),
                         (off[2], KV_WIDTH, "none", 1.0)), BF16, rope_att, transpose_out=True, norm_gain=g_norm)
    q_t, kv_t = qkv_t, qkv_t[ATT_WIDTH:]
    att_gate = _project("att_gate", h, w_mid, off[3] - mid0, ATT_WIDTH, ((0, ATT_WIDTH, "silu", 1.0),), BF16)
    q_idx_t = _project("q_idx", h, w_mid, off[4] - mid0, IDX_HEADS * IDX_DIM,
                       ((0, IDX_HEADS * IDX_DIM, "rope64", 1.0),), BF16, rope_idx, transpose_out=True)
    misc_t = _project("idx_misc", h, w_mid, off[5] - mid0, LANES,
                      ((0, LANES, "idx_misc", IDX_HEADS ** -0.5 * IDX_DIM ** -0.5),), F32, rope_idx,
                      transpose_out=True)
    u_gate = _project("ssm", h, w_tail, 0, 2 * SSM_WIDTH,
                      ((0, SSM_WIDTH, "none", 1.0), (SSM_WIDTH, SSM_WIDTH, "silu", 1.0)), F32)
    gates = _project("merge", h, w_tail, 2 * SSM_WIDTH, N_BRANCH * D_MODEL, ((0, 1024, "sigmoid", 1.0),), BF16,
                     bias=b_merge, tm=1024, tn=1024)

    k_idx = misc_t[:IDX_DIM].T.astype(BF16)
    w_idx_t = misc_t[IDX_DIM:IDX_DIM + IDX_HEADS]
    att = _dsa(q_t, kv_t[:KV_WIDTH], kv_t[KV_WIDTH:], q_idx_t, k_idx, w_idx_t, BF16)
    y_ssm = _s5(u_gate[:, :SSM_WIDTH], a_re, a_im, log_dt, ssm_b_re, ssm_b_im, ssm_c_re, ssm_c_im)
    return _out_stage(att, att_gate, y_ssm, u_gate, gates, x, ssm_d, w_att_out, w_glu, b_glu, w_ssm_out, w_out, g_out)


def kernel(x, positions, g_norm, w_in, b_merge, a_re, a_im, log_dt, ssm_b_re, ssm_b_im, ssm_c_re, ssm_c_im,
           ssm_d, w_glu, b_glu, w_att_out, w_ssm_out, w_out, g_final):
    depth = w_in.shape[0]
    assert depth == 1 and x.shape[0] == 1, "single layer, single sequence"
    y = _layer(x[0], positions[0], g_norm[0], w_in[0], b_merge[0], a_re[0], a_im[0], log_dt[0], ssm_b_re[0],
               ssm_b_im[0], ssm_c_re[0], ssm_c_im[0], ssm_d[0], w_glu[0], b_glu[0], w_att_out[0], w_ssm_out[0],
               w_out[0], g_final)
    return y[None]
```

```python
import functools
import math

import jax
import jax.numpy as jnp
from jax import lax
from jax.experimental import pallas as pl
from jax.experimental.pallas import tpu as pltpu

D_MODEL = 2048
CHUNK = 64
EPS = 1e-6
ATT_HEADS = 8
ATT_KV_HEADS = 2
HEAD_DIM = 128
ATT_WIDTH = ATT_HEADS * HEAD_DIM
KV_WIDTH = ATT_KV_HEADS * HEAD_DIM
IDX_HEADS = 16
IDX_DIM = 64
INDEX_TOPK = 256
ROPE_THETA = 500000.0
ROPE_FRACTION = 4
SSM_GROUP_CH = 16
SSM_GROUPS = 32
SSM_WIDTH = SSM_GROUPS * SSM_GROUP_CH
SSM_STATE = 64
N_BRANCH = 2
SPLITS = (ATT_WIDTH, KV_WIDTH, KV_WIDTH, ATT_WIDTH, IDX_HEADS * IDX_DIM, IDX_DIM, IDX_HEADS,
          SSM_WIDTH, SSM_WIDTH, N_BRANCH * D_MODEL)

LANES = 128
ROPE_FREQS = HEAD_DIM // ROPE_FRACTION // 2
VMEM_LIMIT = 56 * 1024 * 1024

Q_TILE = 256
KEY_TILE = 256
IDX_TILE = 512
ATT_CHUNK = 512
V_ROWS = HEAD_DIM + 16
BLIND_PASSES = 18
SSM_T = 64
LOG2E = 1.4426950408889634
NEG_BIG = -1e30

F32 = jnp.float32
BF16 = jnp.bfloat16


def _params(semantics):
    return pltpu.CompilerParams(dimension_semantics=semantics, vmem_limit_bytes=VMEM_LIMIT)


def _resident(shape):
    zeros = (0,) * len(shape)
    return pl.BlockSpec(shape, lambda *_: zeros, pipeline_mode=pl.Buffered(1))


def _rope_kernel(pos_ref, freq_ref, cos_ref, sin_ref):
    ang = freq_ref[...] * pos_ref[...].astype(F32)
    cos_ref[...] = jnp.cos(ang)
    sin_ref[...] = jnp.sin(ang)


def _rope_tables(positions, head_dim):
    l = positions.shape[0]
    rd = head_dim // ROPE_FRACTION
    half = rd // 2
    inv_freq = jnp.exp(-math.log(ROPE_THETA) * jnp.arange(half, dtype=F32) * (2.0 / rd))
    return pl.pallas_call(
        _rope_kernel,
        out_shape=(jax.ShapeDtypeStruct((half, l), F32),) * 2,
        name="rope_tables_%d" % head_dim,
    )(positions.reshape(1, l), inv_freq.reshape(half, 1))


def _silu(x):
    return x * jax.nn.sigmoid(x)


def _rotate_rows(xt, cos, sin, head_dim):
    half = head_dim // ROPE_FRACTION // 2
    pieces = []
    for base in range(0, LANES, head_dim):
        top, bot = xt[base:base + half], xt[base + half:base + 2 * half]
        pieces += [top * cos - bot * sin, bot * cos + top * sin, xt[base + 2 * half:base + head_dim]]
    return jnp.concatenate(pieces, axis=0)


def _proj_kernel(*refs, segments, has_rope, has_bias, has_norm, transpose_out):
    h_ref, w_ref = refs[0], refs[1]
    pos = 2
    if has_norm:
        gain_ref = refs[pos]
        pos += 1
    if has_bias:
        bias_ref = refs[pos]
        pos += 1
    if has_rope:
        cos, sin = refs[pos][...], refs[pos + 1][...]
        pos += 2
    o_ref = refs[pos]
    if has_norm:
        x = h_ref[...]
        h = (x * lax.rsqrt(jnp.mean(x * x, axis=-1, keepdims=True) + EPS) * gain_ref[...]).astype(BF16)
        refs[pos + 1][...] = h
    else:
        h = h_ref[...]
    acc = jnp.dot(h, w_ref[...], preferred_element_type=F32)
    if has_bias:
        acc = acc + bias_ref[...]
    for start, size, kind, scale in segments:
        for b in range(size // LANES):
            lo = start + b * LANES
            xb = acc[:, lo:lo + LANES]
            if kind == "silu":
                xb = _silu(xb)
            elif kind == "sigmoid":
                xb = jax.nn.sigmoid(xb)
            if not transpose_out:
                assert kind in ("none", "silu", "sigmoid") and scale == 1.0
                o_ref[:, lo:lo + LANES] = xb.astype(o_ref.dtype)
                continue
            xt = xb.T
            if kind == "rope128":
                xt = _rotate_rows(xt, cos, sin, HEAD_DIM)
            elif kind == "rope64":
                xt = _rotate_rows(xt, cos, sin, IDX_DIM)
            elif kind == "idx_misc":
                xt = jnp.concatenate([_rotate_rows(xt, cos, sin, IDX_DIM)[:IDX_DIM], xt[IDX_DIM:] * scale], axis=0)
            if kind != "idx_misc" and scale != 1.0:
                xt = xt * scale
            o_ref[lo:lo + LANES, :] = xt.astype(o_ref.dtype)


def _project(name, h, w, col0, n, segments, out_dtype, rope=None, bias=None, tm=512, tn=None,
             transpose_out=False, norm_gain=None):
    l, d = h.shape
    tn = n if tn is None else tn
    assert col0 % tn == 0 and n % tn == 0
    assert norm_gain is None or n == tn
    jb = col0 // tn
    if transpose_out:
        out_spec = pl.BlockSpec((tn, tm), lambda i, j: (j, i))
        out_shape = jax.ShapeDtypeStruct((n, l), out_dtype)
    else:
        out_spec = pl.BlockSpec((tm, tn), lambda i, j: (i, j))
        out_shape = jax.ShapeDtypeStruct((l, n), out_dtype)
    in_specs = [pl.BlockSpec((tm, d), lambda i, j: (i, 0)), pl.BlockSpec((d, tn), lambda i, j: (0, jb + j))]
    args = [h, w]
    if norm_gain is not None:
        in_specs.append(pl.BlockSpec((1, d), lambda i, j: (0, 0)))
        args.append(norm_gain.reshape(1, d))
        out_spec = (out_spec, pl.BlockSpec((tm, d), lambda i, j: (i, 0)))
        out_shape = (out_shape, jax.ShapeDtypeStruct((l, d), BF16))
    if bias is not None:
        in_specs.append(pl.BlockSpec((1, tn), lambda i, j: (0, j)))
        args.append(bias.reshape(1, n))
    if rope is not None:
        assert transpose_out
        in_specs += [pl.BlockSpec((rope[0].shape[0], tm), lambda i, j: (0, i))] * 2
        args += list(rope)
    kern = functools.partial(_proj_kernel, segments=segments, has_rope=rope is not None, has_bias=bias is not None,
                             has_norm=norm_gain is not None, transpose_out=transpose_out)
    return pl.pallas_call(
        kern,
        grid=(l // tm, n // tn),
        in_specs=in_specs,
        out_specs=out_spec,
        out_shape=out_shape,
        compiler_params=_params(("parallel", "arbitrary")),
        name="in_proj_" + name,
    )(*args)


def _col_reduce(x, op):
    r, q = x.shape
    y = op(x.reshape(r // 64, 64, q), axis=0) if r > 64 else x
    y = op(y.reshape(8, 8, q), axis=0)
    return op(y, axis=0, keepdims=True)


def _dsa_kernel(qt_ref, qit_ref, wt_ref, kidx_ref, k_ref, vt_ref, o_ref,
                sc_ref, s_ref, acc_ref, *, topk, n_att_chunks):
    i = pl.program_id(0)
    n_blk = i + 1
    sub = 8
    grp = ATT_HEADS // ATT_KV_HEADS
    inf = float("inf")

    def score_block(t, carry, masked):
        smin, smax = carry
        start = pl.multiple_of(t * IDX_TILE, IDX_TILE)
        rows = pl.ds(start, IDX_TILE)
        kt = kidx_ref[rows, :]
        acc = jnp.zeros((IDX_TILE, Q_TILE), F32)
        for h in range(IDX_HEADS):
            logit = jnp.dot(kt, qit_ref[h * IDX_DIM:(h + 1) * IDX_DIM, :], preferred_element_type=F32)
            acc = acc + wt_ref[h:h + 1, :] * jnp.maximum(logit, 0.0)
        lo_src = acc
        if masked:
            key_chunk = (start + lax.broadcasted_iota(jnp.int32, acc.shape, 0)) // CHUNK
            q_chunk = i * (Q_TILE // CHUNK) + lax.broadcasted_iota(jnp.int32, acc.shape, 1) // CHUNK
            lo_src = jnp.where(key_chunk <= q_chunk, acc, inf)
            acc = jnp.where(key_chunk <= q_chunk, acc, -inf)
        sc_ref[rows, :] = acc
        return (jnp.minimum(smin, jnp.min(lo_src.reshape(IDX_TILE // sub, sub, Q_TILE), axis=0)),
                jnp.maximum(smax, jnp.max(acc.reshape(IDX_TILE // sub, sub, Q_TILE), axis=0)))

    n_full = (i * Q_TILE) // IDX_TILE
    carry = lax.fori_loop(0, n_full, lambda t, c: score_block(t, c, False),
                          (jnp.full((sub, Q_TILE), inf, F32), jnp.full((sub, Q_TILE), -inf, F32)))
    smin, smax = score_block(n_full, carry, True)
    smin = jnp.min(smin, axis=0, keepdims=True)
    smax = jnp.max(smax, axis=0, keepdims=True)

    blk_per_it = 2 * ATT_CHUNK // KEY_TILE
    n_it = (n_blk + blk_per_it - 1) // blk_per_it

    def fill_body(t, carry):
        sc_ref[pl.ds(pl.multiple_of(t * KEY_TILE, KEY_TILE), KEY_TILE), :] = jnp.full(
            (KEY_TILE, Q_TILE), -inf, F32)
        return carry

    lax.fori_loop(n_blk, n_it * blk_per_it, fill_body, 0)

    def count_ge(cand):
        def count_rows(start, rows, accs):
            blk = sc_ref[pl.ds(pl.multiple_of(start, KEY_TILE), rows), :]
            hit = (blk >= cand).astype(jnp.int32)
            return accs + jnp.sum(hit.reshape(4, rows // (4 * sub), sub, Q_TILE), axis=1)

        accs = lax.fori_loop(0, n_blk // 2, lambda t, a: count_rows(t * 2 * KEY_TILE, 2 * KEY_TILE, a),
                             jnp.zeros((4, sub, Q_TILE), jnp.int32))
        accs = lax.fori_loop(2 * (n_blk // 2), n_blk, lambda t, a: count_rows(t * KEY_TILE, KEY_TILE, a), accs)
        return jnp.sum(jnp.sum(accs, axis=0), axis=0, keepdims=True)

    lane_chunk = lax.broadcasted_iota(jnp.int32, (1, Q_TILE), 1) // CHUNK
    n_adm = (i * (Q_TILE // CHUNK) + lane_chunk + 1) * CHUNK
    hi0 = smax + (jnp.abs(smax) * 1e-6 + 1e-30)

    def bis_step(state):
        lo, hi, cnt_lo, cnt_hi, live_i = state
        mid = lo + 0.5 * (hi - lo)
        live = (live_i > 0) & (mid > lo) & (mid < hi)
        cnt = count_ge(mid)
        ge = cnt >= topk
        lo = jnp.where(live & ge, mid, lo)
        hi = jnp.where(live & ~ge, mid, hi)
        cnt_lo = jnp.where(live & ge, cnt, cnt_lo)
        cnt_hi = jnp.where(live & ~ge, cnt, cnt_hi)
        return lo, hi, cnt_lo, cnt_hi, (live & (cnt_lo > topk)).astype(jnp.int32)

    def bis_body(carry):
        state = bis_step(carry[1:])
        return (jnp.max(state[4]),) + state

    state = lax.fori_loop(0, BLIND_PASSES, lambda _, st: bis_step(st),
                          (smin, hi0, n_adm, jnp.zeros((1, Q_TILE), jnp.int32), (n_adm > topk).astype(jnp.int32)))
    _, thr, _, cnt_lo, cnt_hi, _ = lax.while_loop(lambda c: c[0] > 0, bis_body, (jnp.max(state[4]),) + state)

    tied = cnt_lo > topk

    @pl.when(jnp.max(tied.astype(jnp.int32)) > 0)
    def _():
        keep = (topk - cnt_hi).astype(F32)
        tri = (lax.broadcasted_iota(jnp.int32, (KEY_TILE, KEY_TILE), 0)
               >= lax.broadcasted_iota(jnp.int32, (KEY_TILE, KEY_TILE), 1)).astype(BF16)

        def fix_body(t, before):
            rows = pl.ds(pl.multiple_of(t * KEY_TILE, KEY_TILE), KEY_TILE)
            blk = sc_ref[rows, :]
            eq = (blk == thr) & tied
            eq_f = eq.astype(F32)
            rank = jnp.dot(tri, eq_f.astype(BF16), preferred_element_type=F32) - 1.0 + before
            sc_ref[rows, :] = jnp.where(eq & (rank >= keep), -inf, blk)
            return before + jnp.sum(eq_f, axis=0, keepdims=True)

        lax.fori_loop(0, n_blk, fix_body, jnp.zeros((1, Q_TILE), F32))

    last_chunk = n_att_chunks - 1

    def qk(h, c):
        rows = pl.ds(pl.multiple_of(c * ATT_CHUNK, ATT_CHUNK), ATT_CHUNK)
        return jnp.dot(k_ref[h // grp, rows, :], qt_ref[h * HEAD_DIM:(h + 1) * HEAD_DIM, :],
                       preferred_element_type=F32)

    for kv in range(ATT_KV_HEADS):
        heads = range(kv * grp, (kv + 1) * grp)
        acc_ref[...] = jnp.zeros(acc_ref.shape, F32)
        for g, h in enumerate(heads):
            s_ref[0, g] = qk(h, 0)

        def att_body(it, carry, kv=kv, heads=heads):
            ms, ls = list(carry[0]), list(carry[1])
            for si in range(2):
                c = it * 2 + si
                rows = pl.ds(pl.multiple_of(c * ATT_CHUNK, ATT_CHUNK), ATT_CHUNK)
                bias = jnp.where(sc_ref[rows, :] >= thr, 0.0, NEG_BIG)
                v_t = vt_ref[c, kv * V_ROWS:(kv + 1) * V_ROWS, :]
                for g, h in enumerate(heads):
                    s = s_ref[si, g] + bias
                    s_ref[1 - si, g] = qk(h, jnp.minimum(c + 1, last_chunk))
                    m_new = jnp.maximum(ms[g], _col_reduce(s, jnp.max))
                    alpha = jnp.exp2(ms[g] - m_new)
                    p = jnp.exp2(s - m_new)
                    pv = jnp.dot(v_t, p.astype(BF16), preferred_element_type=F32)
                    ls[g] = alpha * ls[g] + pv[HEAD_DIM:HEAD_DIM + 1]
                    acc_ref[g] = alpha * acc_ref[g] + pv[:HEAD_DIM]
                    ms[g] = m_new
            return tuple(ms), tuple(ls)

        m0 = tuple(jnp.full((1, Q_TILE), NEG_BIG, F32) for _ in heads)
        l0 = tuple(jnp.zeros((1, Q_TILE), F32) for _ in heads)
        _, ls = lax.fori_loop(0, n_it, att_body, (m0, l0))
        for g, h in enumerate(heads):
            o_ref[:, h * HEAD_DIM:(h + 1) * HEAD_DIM] = (acc_ref[g] / ls[g]).T.astype(o_ref.dtype)


def _dsa(q_t, k_t, v_t, q_idx_t, k_idx, w_idx_t, out_dtype):
    l = q_t.shape[1]
    topk = min(INDEX_TOPK, l // 4)
    assert l % (2 * ATT_CHUNK) == 0
    n_att_chunks = l // ATT_CHUNK
    k3 = k_t.reshape(ATT_KV_HEADS, HEAD_DIM, l).transpose(0, 2, 1)
    vt = v_t.reshape(ATT_KV_HEADS, HEAD_DIM, n_att_chunks, ATT_CHUNK)
    vt = jnp.concatenate([vt, jnp.ones((ATT_KV_HEADS, V_ROWS - HEAD_DIM, n_att_chunks, ATT_CHUNK), vt.dtype)], axis=1)
    vt = vt.reshape(ATT_KV_HEADS * V_ROWS, n_att_chunks, ATT_CHUNK).transpose(1, 0, 2)
    kern = functools.partial(_dsa_kernel, topk=topk, n_att_chunks=n_att_chunks)
    return pl.pallas_call(
        kern,
        grid=(l // Q_TILE,),
        in_specs=[
            pl.BlockSpec((ATT_WIDTH, Q_TILE), lambda i: (0, i)),
            pl.BlockSpec((IDX_HEADS * IDX_DIM, Q_TILE), lambda i: (0, i)),
            pl.BlockSpec((IDX_HEADS, Q_TILE), lambda i: (0, i)),
            _resident((l, IDX_DIM)),
            _resident((ATT_KV_HEADS, l, HEAD_DIM)),
            _resident((n_att_chunks, ATT_KV_HEADS * V_ROWS, ATT_CHUNK)),
        ],
        out_specs=pl.BlockSpec((Q_TILE, ATT_WIDTH), lambda i: (i, 0)),
        out_shape=jax.ShapeDtypeStruct((l, ATT_WIDTH), out_dtype),
        scratch_shapes=[
            pltpu.VMEM((l, Q_TILE), F32),
            pltpu.VMEM((2, ATT_HEADS // ATT_KV_HEADS, ATT_CHUNK, Q_TILE), F32),
            pltpu.VMEM((ATT_HEADS // ATT_KV_HEADS, HEAD_DIM, Q_TILE), F32),
        ],
        compiler_params=_params(("arbitrary",)),
        name="dsa",
    )(q_t, q_idx_t, w_idx_t, k_idx, k3, vt)


def _s5_kernel(x_ref, bcat_ref, gcat_ref, per_ref, pei_ref, qre_ref, qim_ref, lam_ref, y_ref,
               mt_ref, er_ref, ei_ref, sr_ref, si_ref):
    r0 = jnp.dot(bcat_ref[0], gcat_ref[0], precision=lax.Precision.HIGHEST, preferred_element_type=F32)
    lane = lax.broadcasted_iota(jnp.int32, r0.shape, 1)
    for s in range(SSM_T):
        shifted = r0 if s == 0 else pltpu.roll(r0, s * SSM_GROUP_CH, 1)
        mt_ref[s * SSM_GROUP_CH:(s + 1) * SSM_GROUP_CH, :] = jnp.where(
            lane >= s * SSM_GROUP_CH, shifted, 0.0).astype(BF16)
    xb = x_ref[0]
    er_ref[...] = jnp.dot(xb, per_ref[0], preferred_element_type=F32)
    ei_ref[...] = jnp.dot(xb, pei_ref[0], preferred_element_type=F32)
    lam_r = lam_ref[0, 0:1, :]
    lam_i = lam_ref[0, 1:2, :]
    n_chunks = xb.shape[0]

    def scan_body(c, carry):
        s_r, s_i = carry
        sr_ref[pl.ds(c, 1), :] = s_r
        si_ref[pl.ds(c, 1), :] = s_i
        e_r = er_ref[pl.ds(c, 1), :]
        e_i = ei_ref[pl.ds(c, 1), :]
        return (lam_r * s_r - lam_i * s_i + e_r, lam_r * s_i + lam_i * s_r + e_i)

    zero = jnp.zeros((1, SSM_STATE), F32)
    lax.fori_loop(0, n_chunks, scan_body, (zero, zero))
    y = jnp.dot(xb, mt_ref[...], preferred_element_type=F32)
    y = y + jnp.dot(sr_ref[...].astype(BF16), qre_ref[0], preferred_element_type=F32)
    y = y + jnp.dot(si_ref[...].astype(BF16), qim_ref[0], preferred_element_type=F32)
    y_ref[0] = y.astype(y_ref.dtype)


def _s5(u, a_re, a_im, log_dt, b_re, b_im, c_re, c_im):
    l = u.shape[0]
    t, c, p, g = SSM_T, SSM_GROUP_CH, SSM_STATE, SSM_GROUPS
    n_chunks = l // t
    width = t * c
    dt = jnp.exp(log_dt.astype(F32))[:, None]
    ar, ai = a_re.astype(F32), a_im.astype(F32)
    mag = jnp.exp(ar * dt)
    lb_re, lb_im = mag * jnp.cos(ai * dt), mag * jnp.sin(ai * dt)
    den = ar * ar + ai * ai
    coef_re = ((lb_re - 1.0) * ar + lb_im * ai) / den
    coef_im = (lb_im * ar - (lb_re - 1.0) * ai) / den
    br, bi = b_re.astype(F32), b_im.astype(F32)
    bb_re = coef_re[..., None] * br - coef_im[..., None] * bi
    bb_im = coef_re[..., None] * bi + coef_im[..., None] * br
    cr, ci = c_re.astype(F32), c_im.astype(F32)
    pr, pi = jnp.ones((g, 1, p), F32), jnp.zeros((g, 1, p), F32)
    sr, si = lb_re[:, None, :], lb_im[:, None, :]
    while pr.shape[1] < t + 1:
        pr, pi = (jnp.concatenate([pr, pr * sr - pi * si], axis=1),
                  jnp.concatenate([pi, pr * si + pi * sr], axis=1))
        sr, si = sr * sr - si * si, 2.0 * sr * si
    pr, pi = pr[:, :t + 1], pi[:, :t + 1]
    cr_t, ci_t = cr.transpose(0, 2, 1)[:, :, None, :], ci.transpose(0, 2, 1)[:, :, None, :]
    pr_t, pi_t = pr.transpose(0, 2, 1)[:, :, :, None], pi.transpose(0, 2, 1)[:, :, :, None]
    gc_re = cr_t * pr_t - ci_t * pi_t
    gc_im = cr_t * pi_t + ci_t * pr_t
    tb_re, tb_im = bb_re.transpose(0, 2, 1), bb_im.transpose(0, 2, 1)
    bcat = jnp.concatenate([tb_re, -tb_im], axis=2)
    gcat = jnp.concatenate([gc_re[:, :, :t], gc_im[:, :, :t]], axis=1).reshape(g, 2 * p, width)
    rr, ri = pr[:, :t][:, ::-1][:, :, None, :], pi[:, :t][:, ::-1][:, :, None, :]
    pe_re = (rr * tb_re[:, None] - ri * tb_im[:, None]).reshape(g, width, p)
    pe_im = (rr * tb_im[:, None] + ri * tb_re[:, None]).reshape(g, width, p)
    q_re = gc_re[:, :, 1:t + 1].reshape(g, p, width)
    q_im = gc_im[:, :, 1:t + 1].reshape(g, p, width)
    lam_pack = jnp.stack([pr[:, t], pi[:, t]], axis=1)
    x = u.astype(BF16).reshape(n_chunks, t, g, c).transpose(2, 0, 1, 3).reshape(g, n_chunks, width)
    grp3 = lambda a, b: pl.BlockSpec((1, a, b), lambda i: (i, 0, 0))
    y = pl.pallas_call(
        _s5_kernel,
        grid=(g,),
        in_specs=[grp3(n_chunks, width), grp3(c, 2 * p), grp3(2 * p, width), grp3(width, p), grp3(width, p),
                  grp3(p, width), grp3(p, width), grp3(2, p)],
        out_specs=grp3(n_chunks, width),
        out_shape=jax.ShapeDtypeStruct((g, n_chunks, width), BF16),
        scratch_shapes=[
            pltpu.VMEM((width, width), BF16),
            pltpu.VMEM((n_chunks, p), F32),
            pltpu.VMEM((n_chunks, p), F32),
            pltpu.VMEM((n_chunks, p), F32),
            pltpu.VMEM((n_chunks, p), F32),
        ],
        compiler_params=_params(("parallel",)),
        name="s5",
    )(x, bcat, gcat, pe_re.astype(BF16), pe_im.astype(BF16), q_re.astype(BF16), (-q_im).astype(BF16), lam_pack)
    return y.reshape(g, n_chunks, t, c).transpose(1, 2, 0, 3).reshape(l, SSM_WIDTH)


def _gelu_tanh(x):
    return 0.5 * x * (1.0 + jnp.tanh(math.sqrt(2.0 / math.pi) * (x + 0.044715 * (x * x * x))))


def _out_kernel(att_ref, ag_ref, ys_ref, u_ref, sg_ref, gate_ref, x_ref, d_ref, watt_ref, wglu_ref, bglu_ref,
                wssm_ref, wout_ref, gfin_ref, o_ref):
    a = (att_ref[...].astype(F32) * ag_ref[...].astype(F32)).astype(BF16)
    y_a = jnp.dot(a, watt_ref[...], preferred_element_type=F32)
    s = _gelu_tanh(ys_ref[...].astype(F32) + d_ref[...] * u_ref[...])
    z = jnp.dot(s.astype(BF16), wglu_ref[...], preferred_element_type=F32) + bglu_ref[...]
    s = s * jax.nn.sigmoid(z)
    y_b = jnp.dot((s * sg_ref[...]).astype(BF16), wssm_ref[...], preferred_element_type=F32)
    gates = gate_ref[...].astype(F32)
    mixed = gates[:, :D_MODEL] * y_a + gates[:, D_MODEL:] * y_b
    r = x_ref[...] + jnp.dot(mixed.astype(BF16), wout_ref[...], preferred_element_type=F32)
    y = r * lax.rsqrt(jnp.mean(r * r, axis=-1, keepdims=True) + EPS)
    o_ref[...] = (y * gfin_ref[...]).astype(o_ref.dtype)


def _out_stage(att, att_gate, y_ssm, u_gate, gates, x, d_skip, w_att_out, w_glu, b_glu, w_ssm_out, w_out,
               g_final, tm=256):
    l = x.shape[0]
    row = lambda n: pl.BlockSpec((tm, n), lambda i: (i, 0))
    return pl.pallas_call(
        _out_kernel,
        grid=(l // tm,),
        in_specs=[row(ATT_WIDTH), row(ATT_WIDTH), row(SSM_WIDTH), row(SSM_WIDTH),
                  pl.BlockSpec((tm, SSM_WIDTH), lambda i: (i, 1)),
                  row(N_BRANCH * D_MODEL), row(D_MODEL), _resident((1, SSM_WIDTH)),
                  _resident((ATT_WIDTH, D_MODEL)), _resident((SSM_WIDTH, SSM_WIDTH)), _resident((1, SSM_WIDTH)),
                  _resident((SSM_WIDTH, D_MODEL)), _resident((D_MODEL, D_MODEL)), _resident((1, D_MODEL))],
        out_specs=row(D_MODEL),
        out_shape=jax.ShapeDtypeStruct((l, D_MODEL), x.dtype),
        compiler_params=_params(("parallel",)),
        name="out_stage",
    )(att, att_gate, y_ssm, u_gate, u_gate, gates, x, d_skip.astype(F32).reshape(1, -1), w_att_out.astype(BF16),
      w_glu.astype(BF16), b_glu.reshape(1, -1), w_ssm_out.astype(BF16), w_out.astype(BF16), g_final.reshape(1, -1))


def _layer(x, positions, g_norm, w_in, b_merge, a_re, a_im, log_dt, ssm_b_re, ssm_b_im, ssm_c_re, ssm_c_im,
           ssm_d, w_glu, b_glu, w_att_out, w_ssm_out, w_out, g_out):
    off = [0]
    for s in SPLITS:
        off.append(off[-1] + s)
    head_end = off[7] + LANES - IDX_DIM - IDX_HEADS
    w_qkv = w_in[:, :off[3]].astype(BF16)
    mid0 = off[3] - ATT_WIDTH
    w_mid = w_in[:, mid0:head_end].astype(BF16)
    w_tail = w_in[:, off[7]:].astype(BF16)
    rope_att = _rope_tables(positions, HEAD_DIM)
    rope_idx = _rope_tables(positions, IDX_DIM)

    q_scale = HEAD_DIM ** -0.5 * LOG2E
    qkv_t, h = _project("qkv", x, w_qkv, 0, off[3],
                        ((off[0], ATT_WIDTH, "rope128", q_scale), (off[1], KV_WIDTH, "rope128", 1.0),
                         (off[2], KV_WIDTH, "none", 1.0)), BF16, rope_att, transpose_out=True, norm_gain=g_norm)
    q_t, kv_t = qkv_t, qkv_t[ATT_WIDTH:]
    att_gate = _project("att_gate", h, w_mid, off[3] - mid0, ATT_WIDTH, ((0, ATT_WIDTH, "silu", 1.0),), BF16)
    q_idx_t = _project("q_idx", h, w_mid, off[4] - mid0, IDX_HEADS * IDX_DIM,
                       ((0, IDX_HEADS * IDX_DIM, "rope64", 1.0),), BF16, rope_idx, transpose_out=True)
    misc_t = _project("idx_misc", h, w_mid, off[5] - mid0, LANES,
                      ((0, LANES, "idx_misc", IDX_HEADS ** -0.5 * IDX_DIM ** -0.5),), F32, rope_idx,
                      transpose_out=True)
    u_gate = _project("ssm", h, w_tail, 0, 2 * SSM_WIDTH,
                      ((0, SSM_WIDTH, "none", 1.0), (SSM_WIDTH, SSM_WIDTH, "silu", 1.0)), F32)
    gates = _project("merge", h, w_tail, 2 * SSM_WIDTH, N_BRANCH * D_MODEL, ((0, 1024, "sigmoid", 1.0),), BF16,
                     bias=b_merge, tm=1024, tn=1024)

    k_idx = misc_t[:IDX_DIM].T.astype(BF16)
    w_idx_t = misc_t[IDX_DIM:IDX_DIM + IDX_HEADS]
    att = _dsa(q_t, kv_t[:KV_WIDTH], kv_t[KV_WIDTH:], q_idx_t, k_idx, w_idx_t, BF16)
    y_ssm = _s5(u_gate[:, :SSM_WIDTH], a_re, a_im, log_dt, ssm_b_re, ssm_b_im, ssm_c_re, ssm_c_im)
    return _out_stage(att, att_gate, y_ssm, u_gate, gates, x, ssm_d, w_att_out, w_glu, b_glu, w_ssm_out, w_out, g_out)


def kernel(x, positions, g_norm, w_in, b_merge, a_re, a_im, log_dt, ssm_b_re, ssm_b_im, ssm_c_re, ssm_c_im,
           ssm_d, w_glu, b_glu, w_att_out, w_ssm_out, w_out, g_final):
    depth = w_in.shape[0]
    assert depth == 1 and x.shape[0] == 1, "single layer, single sequence"
    y = _layer(x[0], positions[0], g_norm[0], w_in[0], b_merge[0], a_re[0], a_im[0], log_dt[0], ssm_b_re[0],
               ssm_b_im[0], ssm_c_re[0], ssm_c_im[0], ssm_d[0], w_glu[0], b_glu[0], w_att_out[0], w_ssm_out[0],
               w_out[0], g_final)
    return y[None]
```

```python
import functools
import math

import jax
import jax.numpy as jnp
from jax import lax
from jax.experimental import pallas as pl
from jax.experimental.pallas import tpu as pltpu

D_MODEL = 2048
CHUNK = 64
EPS = 1e-6
ATT_HEADS = 8
ATT_KV_HEADS = 2
HEAD_DIM = 128
ATT_WIDTH = ATT_HEADS * HEAD_DIM
KV_WIDTH = ATT_KV_HEADS * HEAD_DIM
IDX_HEADS = 16
IDX_DIM = 64
INDEX_TOPK = 256
ROPE_THETA = 500000.0
ROPE_FRACTION = 4
SSM_GROUP_CH = 16
SSM_GROUPS = 32
SSM_WIDTH = SSM_GROUPS * SSM_GROUP_CH
SSM_STATE = 64
N_BRANCH = 2
SPLITS = (ATT_WIDTH, KV_WIDTH, KV_WIDTH, ATT_WIDTH, IDX_HEADS * IDX_DIM, IDX_DIM, IDX_HEADS,
          SSM_WIDTH, SSM_WIDTH, N_BRANCH * D_MODEL)

LANES = 128
ROPE_FREQS = HEAD_DIM // ROPE_FRACTION // 2
VMEM_LIMIT = 56 * 1024 * 1024

Q_TILE = 256
KEY_TILE = 256
IDX_TILE = 512
ATT_CHUNK = 512
V_ROWS = HEAD_DIM + 16
BLIND_PASSES = 18
SSM_T = 64
LOG2E = 1.4426950408889634
NEG_BIG = -1e30

F32 = jnp.float32
BF16 = jnp.bfloat16


def _params(semantics):
    return pltpu.CompilerParams(dimension_semantics=semantics, vmem_limit_bytes=VMEM_LIMIT)


def _resident(shape):
    zeros = (0,) * len(shape)
    return pl.BlockSpec(shape, lambda *_: zeros, pipeline_mode=pl.Buffered(1))


def _rope_kernel(pos_ref, freq_ref, cos_ref, sin_ref):
    ang = freq_ref[...] * pos_ref[...].astype(F32)
    cos_ref[...] = jnp.cos(ang)
    sin_ref[...] = jnp.sin(ang)


def _rope_tables(positions, head_dim):
    l = positions.shape[0]
    rd = head_dim // ROPE_FRACTION
    half = rd // 2
    inv_freq = jnp.exp(-math.log(ROPE_THETA) * jnp.arange(half, dtype=F32) * (2.0 / rd))
    return pl.pallas_call(
        _rope_kernel,
        out_shape=(jax.ShapeDtypeStruct((half, l), F32),) * 2,
        name="rope_tables_%d" % head_dim,
    )(positions.reshape(1, l), inv_freq.reshape(half, 1))


def _silu(x):
    return x * jax.nn.sigmoid(x)


def _rotate_rows(xt, cos, sin, head_dim):
    half = head_dim // ROPE_FRACTION // 2
    pieces = []
    for base in range(0, LANES, head_dim):
        top, bot = xt[base:base + half], xt[base + half:base + 2 * half]
        pieces += [top * cos - bot * sin, bot * cos + top * sin, xt[base + 2 * half:base + head_dim]]
    return jnp.concatenate(pieces, axis=0)


def _proj_kernel(*refs, segments, has_rope, has_bias, has_norm, transposed):
    h_ref, w_ref = refs[0], refs[1]
    pos = 2
    if has_norm:
        gain_ref = refs[pos]
        pos += 1
    if has_bias:
        bias_ref = refs[pos]
        pos += 1
    if has_rope:
        cos, sin = refs[pos][...], refs[pos + 1][...]
        pos += 2
    o_refs = refs[pos:pos + len(transposed)]
    if has_norm:
        x = h_ref[...]
        h = (x * lax.rsqrt(jnp.mean(x * x, axis=-1, keepdims=True) + EPS) * gain_ref[...]).astype(BF16)
        refs[pos + len(transposed)][...] = h
    else:
        h = h_ref[...]
    acc = jnp.dot(h, w_ref[...], preferred_element_type=F32)
    if has_bias:
        acc = acc + bias_ref[...]
    out_pos = [0] * len(transposed)
    for start, size, kind, scale, k in segments:
        o_ref = o_refs[k]
        for b in range(size // LANES):
            xb = acc[:, start + b * LANES:start + (b + 1) * LANES]
            lo = out_pos[k]
            out_pos[k] += LANES
            if kind == "silu":
                xb = _silu(xb)
            elif kind == "sigmoid":
                xb = jax.nn.sigmoid(xb)
            if not transposed[k]:
                assert kind in ("none", "silu", "sigmoid") and scale == 1.0
                o_ref[:, lo:lo + LANES] = xb.astype(o_ref.dtype)
                continue
            xt = xb.T
            if kind == "rope128":
                xt = _rotate_rows(xt, cos, sin, HEAD_DIM)
            elif kind == "rope64":
                xt = _rotate_rows(xt, cos, sin, IDX_DIM)
            elif kind == "idx_misc":
                xt = jnp.concatenate([_rotate_rows(xt, cos, sin, IDX_DIM)[:IDX_DIM], xt[IDX_DIM:] * scale], axis=0)
            if kind != "idx_misc" and scale != 1.0:
                xt = xt * scale
            o_ref[lo:lo + LANES, :] = xt.astype(o_ref.dtype)


def _project(name, h, w, col0, n, segments, out_dtype, rope=None, bias=None, tm=512, tn=None,
             transpose_out=False, norm_gain=None):
    l, d = h.shape
    tn = n if tn is None else tn
    assert col0 % tn == 0 and n % tn == 0
    assert norm_gain is None or n == tn
    jb = col0 // tn
    single = not isinstance(out_dtype, tuple)
    dtypes = (out_dtype,) if single else out_dtype
    transposed = (transpose_out,) if single else transpose_out
    segments = tuple(seg if len(seg) == 5 else seg + (0,) for seg in segments)
    assert single or n == tn
    out_spec, out_shape = [], []
    for k, (dt, tr) in enumerate(zip(dtypes, transposed)):
        wk = tn if single else sum(seg[1] for seg in segments if seg[4] == k)
        nk = n if single else wk
        if tr:
            out_spec.append(pl.BlockSpec((wk, tm), lambda i, j: (j, i)))
            out_shape.append(jax.ShapeDtypeStruct((nk, l), dt))
        else:
            out_spec.append(pl.BlockSpec((tm, wk), lambda i, j: (i, j)))
            out_shape.append(jax.ShapeDtypeStruct((l, nk), dt))
    in_specs = [pl.BlockSpec((tm, d), lambda i, j: (i, 0)), pl.BlockSpec((d, tn), lambda i, j: (0, jb + j))]
    args = [h, w]
    if norm_gain is not None:
        in_specs.append(pl.BlockSpec((1, d), lambda i, j: (0, 0)))
        args.append(norm_gain.reshape(1, d))
        out_spec.append(pl.BlockSpec((tm, d), lambda i, j: (i, 0)))
        out_shape.append(jax.ShapeDtypeStruct((l, d), BF16))
    out_spec, out_shape = (out_spec[0], out_shape[0]) if len(out_spec) == 1 else (tuple(out_spec), tuple(out_shape))
    if bias is not None:
        in_specs.append(pl.BlockSpec((1, tn), lambda i, j: (0, j)))
        args.append(bias.reshape(1, n))
    if rope is not None:
        in_specs += [pl.BlockSpec((rope[0].shape[0], tm), lambda i, j: (0, i))] * 2
        args += list(rope)
    kern = functools.partial(_proj_kernel, segments=segments, has_rope=rope is not None, has_bias=bias is not None,
                             has_norm=norm_gain is not None, transposed=transposed)
    return pl.pallas_call(
        kern,
        grid=(l // tm, n // tn),
        in_specs=in_specs,
        out_specs=out_spec,
        out_shape=out_shape,
        compiler_params=_params(("parallel", "arbitrary")),
        name="in_proj_" + name,
    )(*args)


def _col_reduce(x, op):
    r, q = x.shape
    y = op(x.reshape(r // 64, 64, q), axis=0) if r > 64 else x
    y = op(y.reshape(8, 8, q), axis=0)
    return op(y, axis=0, keepdims=True)


def _dsa_kernel(qt_ref, qit_ref, wt_ref, kidx_ref, k_ref, vt_ref, o_ref,
                sc_ref, s_ref, acc_ref, *, topk, n_att_chunks):
    i = pl.program_id(0)
    n_blk = i + 1
    sub = 8
    grp = ATT_HEADS // ATT_KV_HEADS
    inf = float("inf")

    def score_block(t, carry, masked):
        smin, smax = carry
        start = pl.multiple_of(t * IDX_TILE, IDX_TILE)
        rows = pl.ds(start, IDX_TILE)
        kt = kidx_ref[rows, :]
        acc = jnp.zeros((IDX_TILE, Q_TILE), F32)
        for h in range(IDX_HEADS):
            logit = jnp.dot(kt, qit_ref[h * IDX_DIM:(h + 1) * IDX_DIM, :], preferred_element_type=F32)
            acc = acc + wt_ref[h:h + 1, :] * jnp.maximum(logit, 0.0)
        lo_src = acc
        if masked:
            key_chunk = (start + lax.broadcasted_iota(jnp.int32, acc.shape, 0)) // CHUNK
            q_chunk = i * (Q_TILE // CHUNK) + lax.broadcasted_iota(jnp.int32, acc.shape, 1) // CHUNK
            lo_src = jnp.where(key_chunk <= q_chunk, acc, inf)
            acc = jnp.where(key_chunk <= q_chunk, acc, -inf)
        sc_ref[rows, :] = acc
        return (jnp.minimum(smin, jnp.min(lo_src.reshape(IDX_TILE // sub, sub, Q_TILE), axis=0)),
                jnp.maximum(smax, jnp.max(acc.reshape(IDX_TILE // sub, sub, Q_TILE), axis=0)))

    n_full = (i * Q_TILE) // IDX_TILE
    carry = lax.fori_loop(0, n_full, lambda t, c: score_block(t, c, False),
                          (jnp.full((sub, Q_TILE), inf, F32), jnp.full((sub, Q_TILE), -inf, F32)))
    smin, smax = score_block(n_full, carry, True)
    smin = jnp.min(smin, axis=0, keepdims=True)
    smax = jnp.max(smax, axis=0, keepdims=True)

    blk_per_it = 2 * ATT_CHUNK // KEY_TILE
    n_it = (n_blk + blk_per_it - 1) // blk_per_it

    def fill_body(t, carry):
        sc_ref[pl.ds(pl.multiple_of(t * KEY_TILE, KEY_TILE), KEY_TILE), :] = jnp.full(
            (KEY_TILE, Q_TILE), -inf, F32)
        return carry

    lax.fori_loop(n_blk, n_it * blk_per_it, fill_body, 0)

    def count_ge(cand):
        def count_rows(start, rows, accs):
            blk = sc_ref[pl.ds(pl.multiple_of(start, KEY_TILE), rows), :]
            hit = (blk >= cand).astype(jnp.int32)
            return accs + jnp.sum(hit.reshape(4, rows // (4 * sub), sub, Q_TILE), axis=1)

        accs = lax.fori_loop(0, n_blk // 2, lambda t, a: count_rows(t * 2 * KEY_TILE, 2 * KEY_TILE, a),
                             jnp.zeros((4, sub, Q_TILE), jnp.int32))
        accs = lax.fori_loop(2 * (n_blk // 2), n_blk, lambda t, a: count_rows(t * KEY_TILE, KEY_TILE, a), accs)
        return jnp.sum(jnp.sum(accs, axis=0), axis=0, keepdims=True)

    lane_chunk = lax.broadcasted_iota(jnp.int32, (1, Q_TILE), 1) // CHUNK
    n_adm = (i * (Q_TILE // CHUNK) + lane_chunk + 1) * CHUNK
    hi0 = smax + (jnp.abs(smax) * 1e-6 + 1e-30)

    def bis_step(state):
        lo, hi, cnt_lo, cnt_hi, live_i = state
        mid = lo + 0.5 * (hi - lo)
        live = (live_i > 0) & (mid > lo) & (mid < hi)
        cnt = count_ge(mid)
        ge = cnt >= topk
        lo = jnp.where(live & ge, mid, lo)
        hi = jnp.where(live & ~ge, mid, hi)
        cnt_lo = jnp.where(live & ge, cnt, cnt_lo)
        cnt_hi = jnp.where(live & ~ge, cnt, cnt_hi)
        return lo, hi, cnt_lo, cnt_hi, (live & (cnt_lo > topk)).astype(jnp.int32)

    def bis_body(carry):
        state = bis_step(carry[1:])
        return (jnp.max(state[4]),) + state

    state = lax.fori_loop(0, BLIND_PASSES, lambda _, st: bis_step(st),
                          (smin, hi0, n_adm, jnp.zeros((1, Q_TILE), jnp.int32), (n_adm > topk).astype(jnp.int32)))
    _, thr, _, cnt_lo, cnt_hi, _ = lax.while_loop(lambda c: c[0] > 0, bis_body, (jnp.max(state[4]),) + state)

    tied = cnt_lo > topk

    @pl.when(jnp.max(tied.astype(jnp.int32)) > 0)
    def _():
        keep = (topk - cnt_hi).astype(F32)
        tri = (lax.broadcasted_iota(jnp.int32, (KEY_TILE, KEY_TILE), 0)
               >= lax.broadcasted_iota(jnp.int32, (KEY_TILE, KEY_TILE), 1)).astype(BF16)

        def fix_body(t, before):
            rows = pl.ds(pl.multiple_of(t * KEY_TILE, KEY_TILE), KEY_TILE)
            blk = sc_ref[rows, :]
            eq = (blk == thr) & tied
            eq_f = eq.astype(F32)
            rank = jnp.dot(tri, eq_f.astype(BF16), preferred_element_type=F32) - 1.0 + before
            sc_ref[rows, :] = jnp.where(eq & (rank >= keep), -inf, blk)
            return before + jnp.sum(eq_f, axis=0, keepdims=True)

        lax.fori_loop(0, n_blk, fix_body, jnp.zeros((1, Q_TILE), F32))

    last_chunk = n_att_chunks - 1

    def qk(h, c):
        rows = pl.ds(pl.multiple_of(c * ATT_CHUNK, ATT_CHUNK), ATT_CHUNK)
        return jnp.dot(k_ref[h // grp, rows, :], qt_ref[h * HEAD_DIM:(h + 1) * HEAD_DIM, :],
                       preferred_element_type=F32)

    for kv in range(ATT_KV_HEADS):
        heads = range(kv * grp, (kv + 1) * grp)
        acc_ref[...] = jnp.zeros(acc_ref.shape, F32)
        for g, h in enumerate(heads):
            s_ref[0, g] = qk(h, 0)

        def att_body(it, carry, kv=kv, heads=heads):
            ms, ls = list(carry[0]), list(carry[1])
            for si in range(2):
                c = it * 2 + si
                rows = pl.ds(pl.multiple_of(c * ATT_CHUNK, ATT_CHUNK), ATT_CHUNK)
                bias = jnp.where(sc_ref[rows, :] >= thr, 0.0, NEG_BIG)
                v_t = vt_ref[c, kv * V_ROWS:(kv + 1) * V_ROWS, :]
                for g, h in enumerate(heads):
                    s = s_ref[si, g] + bias
                    s_ref[1 - si, g] = qk(h, jnp.minimum(c + 1, last_chunk))
                    m_new = jnp.maximum(ms[g], _col_reduce(s, jnp.max))
                    alpha = jnp.exp2(ms[g] - m_new)
                    p = jnp.exp2(s - m_new)
                    pv = jnp.dot(v_t, p.astype(BF16), preferred_element_type=F32)
                    ls[g] = alpha * ls[g] + pv[HEAD_DIM:HEAD_DIM + 1]
                    acc_ref[g] = alpha * acc_ref[g] + pv[:HEAD_DIM]
                    ms[g] = m_new
            return tuple(ms), tuple(ls)

        m0 = tuple(jnp.full((1, Q_TILE), NEG_BIG, F32) for _ in heads)
        l0 = tuple(jnp.zeros((1, Q_TILE), F32) for _ in heads)
        _, ls = lax.fori_loop(0, n_it, att_body, (m0, l0))
        for g, h in enumerate(heads):
            o_ref[:, h * HEAD_DIM:(h + 1) * HEAD_DIM] = (acc_ref[g] / ls[g]).T.astype(o_ref.dtype)


def _dsa(q_t, k_t, v_t, q_idx_t, k_idx, w_idx_t, out_dtype):
    l = q_t.shape[1]
    topk = min(INDEX_TOPK, l // 4)
    assert l % (2 * ATT_CHUNK) == 0
    n_att_chunks = l // ATT_CHUNK
    k3 = k_t.reshape(ATT_KV_HEADS, HEAD_DIM, l).transpose(0, 2, 1)
    vt = v_t.reshape(ATT_KV_HEADS, HEAD_DIM, n_att_chunks, ATT_CHUNK)
    vt = jnp.concatenate([vt, jnp.ones((ATT_KV_HEADS, V_ROWS - HEAD_DIM, n_att_chunks, ATT_CHUNK), vt.dtype)], axis=1)
    vt = vt.reshape(ATT_KV_HEADS * V_ROWS, n_att_chunks, ATT_CHUNK).transpose(1, 0, 2)
    kern = functools.partial(_dsa_kernel, topk=topk, n_att_chunks=n_att_chunks)
    return pl.pallas_call(
        kern,
        grid=(l // Q_TILE,),
        in_specs=[
            pl.BlockSpec((ATT_WIDTH, Q_TILE), lambda i: (0, i)),
            pl.BlockSpec((IDX_HEADS * IDX_DIM, Q_TILE), lambda i: (0, i)),
            pl.BlockSpec((IDX_HEADS, Q_TILE), lambda i: (0, i)),
            _resident((l, IDX_DIM)),
            _resident((ATT_KV_HEADS, l, HEAD_DIM)),
            _resident((n_att_chunks, ATT_KV_HEADS * V_ROWS, ATT_CHUNK)),
        ],
        out_specs=pl.BlockSpec((Q_TILE, ATT_WIDTH), lambda i: (i, 0)),
        out_shape=jax.ShapeDtypeStruct((l, ATT_WIDTH), out_dtype),
        scratch_shapes=[
            pltpu.VMEM((l, Q_TILE), F32),
            pltpu.VMEM((2, ATT_HEADS // ATT_KV_HEADS, ATT_CHUNK, Q_TILE), F32),
            pltpu.VMEM((ATT_HEADS // ATT_KV_HEADS, HEAD_DIM, Q_TILE), F32),
        ],
        compiler_params=_params(("arbitrary",)),
        name="dsa",
    )(q_t, q_idx_t, w_idx_t, k_idx, k3, vt)


def _s5_kernel(x_ref, bcat_ref, gcat_ref, per_ref, pei_ref, qre_ref, qim_ref, lam_ref, y_ref,
               mt_ref, er_ref, ei_ref, sr_ref, si_ref):
    r0 = jnp.dot(bcat_ref[0], gcat_ref[0], precision=lax.Precision.HIGHEST, preferred_element_type=F32)
    lane = lax.broadcasted_iota(jnp.int32, r0.shape, 1)
    for s in range(SSM_T):
        shifted = r0 if s == 0 else pltpu.roll(r0, s * SSM_GROUP_CH, 1)
        mt_ref[s * SSM_GROUP_CH:(s + 1) * SSM_GROUP_CH, :] = jnp.where(
            lane >= s * SSM_GROUP_CH, shifted, 0.0).astype(BF16)
    xb = x_ref[0]
    er_ref[...] = jnp.dot(xb, per_ref[0], preferred_element_type=F32)
    ei_ref[...] = jnp.dot(xb, pei_ref[0], preferred_element_type=F32)
    lam_r = lam_ref[0, 0:1, :]
    lam_i = lam_ref[0, 1:2, :]
    n_chunks = xb.shape[0]

    def scan_body(c, carry):
        s_r, s_i = carry
        sr_ref[pl.ds(c, 1), :] = s_r
        si_ref[pl.ds(c, 1), :] = s_i
        e_r = er_ref[pl.ds(c, 1), :]
        e_i = ei_ref[pl.ds(c, 1), :]
        return (lam_r * s_r - lam_i * s_i + e_r, lam_r * s_i + lam_i * s_r + e_i)

    zero = jnp.zeros((1, SSM_STATE), F32)
    lax.fori_loop(0, n_chunks, scan_body, (zero, zero))
    y = jnp.dot(xb, mt_ref[...], preferred_element_type=F32)
    y = y + jnp.dot(sr_ref[...].astype(BF16), qre_ref[0], preferred_element_type=F32)
    y = y + jnp.dot(si_ref[...].astype(BF16), qim_ref[0], preferred_element_type=F32)
    y_ref[0] = y.astype(y_ref.dtype)


def _s5(u, a_re, a_im, log_dt, b_re, b_im, c_re, c_im):
    l = u.shape[0]
    t, c, p, g = SSM_T, SSM_GROUP_CH, SSM_STATE, SSM_GROUPS
    n_chunks = l // t
    width = t * c
    dt = jnp.exp(log_dt.astype(F32))[:, None]
    ar, ai = a_re.astype(F32), a_im.astype(F32)
    mag = jnp.exp(ar * dt)
    lb_re, lb_im = mag * jnp.cos(ai * dt), mag * jnp.sin(ai * dt)
    den = ar * ar + ai * ai
    coef_re = ((lb_re - 1.0) * ar + lb_im * ai) / den
    coef_im = (lb_im * ar - (lb_re - 1.0) * ai) / den
    br, bi = b_re.astype(F32), b_im.astype(F32)
    bb_re = coef_re[..., None] * br - coef_im[..., None] * bi
    bb_im = coef_re[..., None] * bi + coef_im[..., None] * br
    cr, ci = c_re.astype(F32), c_im.astype(F32)
    pr, pi = jnp.ones((g, 1, p), F32), jnp.zeros((g, 1, p), F32)
    sr, si = lb_re[:, None, :], lb_im[:, None, :]
    while pr.shape[1] < t + 1:
        pr, pi = (jnp.concatenate([pr, pr * sr - pi * si], axis=1),
                  jnp.concatenate([pi, pr * si + pi * sr], axis=1))
        sr, si = sr * sr - si * si, 2.0 * sr * si
    pr, pi = pr[:, :t + 1], pi[:, :t + 1]
    cr_t, ci_t = cr.transpose(0, 2, 1)[:, :, None, :], ci.transpose(0, 2, 1)[:, :, None, :]
    pr_t, pi_t = pr.transpose(0, 2, 1)[:, :, :, None], pi.transpose(0, 2, 1)[:, :, :, None]
    gc_re = cr_t * pr_t - ci_t * pi_t
    gc_im = cr_t * pi_t + ci_t * pr_t
    tb_re, tb_im = bb_re.transpose(0, 2, 1), bb_im.transpose(0, 2, 1)
    bcat = jnp.concatenate([tb_re, -tb_im], axis=2)
    gcat = jnp.concatenate([gc_re[:, :, :t], gc_im[:, :, :t]], axis=1).reshape(g, 2 * p, width)
    rr, ri = pr[:, :t][:, ::-1][:, :, None, :], pi[:, :t][:, ::-1][:, :, None, :]
    pe_re = (rr * tb_re[:, None] - ri * tb_im[:, None]).reshape(g, width, p)
    pe_im = (rr * tb_im[:, None] + ri * tb_re[:, None]).reshape(g, width, p)
    q_re = gc_re[:, :, 1:t + 1].reshape(g, p, width)
    q_im = gc_im[:, :, 1:t + 1].reshape(g, p, width)
    lam_pack = jnp.stack([pr[:, t], pi[:, t]], axis=1)
    x = u.astype(BF16).reshape(n_chunks, t, g, c).transpose(2, 0, 1, 3).reshape(g, n_chunks, width)
    grp3 = lambda a, b: pl.BlockSpec((1, a, b), lambda i: (i, 0, 0))
    y = pl.pallas_call(
        _s5_kernel,
        grid=(g,),
        in_specs=[grp3(n_chunks, width), grp3(c, 2 * p), grp3(2 * p, width), grp3(width, p), grp3(width, p),
                  grp3(p, width), grp3(p, width), grp3(2, p)],
        out_specs=grp3(n_chunks, width),
        out_shape=jax.ShapeDtypeStruct((g, n_chunks, width), BF16),
        scratch_shapes=[
            pltpu.VMEM((width, width), BF16),
            pltpu.VMEM((n_chunks, p), F32),
            pltpu.VMEM((n_chunks, p), F32),
            pltpu.VMEM((n_chunks, p), F32),
            pltpu.VMEM((n_chunks, p), F32),
        ],
        compiler_params=_params(("parallel",)),
        name="s5",
    )(x, bcat, gcat, pe_re.astype(BF16), pe_im.astype(BF16), q_re.astype(BF16), (-q_im).astype(BF16), lam_pack)
    return y.reshape(g, n_chunks, t, c).transpose(1, 2, 0, 3).reshape(l, SSM_WIDTH)


def _gelu_tanh(x):
    return 0.5 * x * (1.0 + jnp.tanh(math.sqrt(2.0 / math.pi) * (x + 0.044715 * (x * x * x))))


def _out_kernel(att_ref, ag_ref, ys_ref, u_ref, sg_ref, gate_ref, x_ref, d_ref, watt_ref, wglu_ref, bglu_ref,
                wssm_ref, wout_ref, gfin_ref, o_ref):
    a = (att_ref[...].astype(F32) * ag_ref[...].astype(F32)).astype(BF16)
    y_a = jnp.dot(a, watt_ref[...], preferred_element_type=F32)
    s = _gelu_tanh(ys_ref[...].astype(F32) + d_ref[...] * u_ref[...])
    z = jnp.dot(s.astype(BF16), wglu_ref[...], preferred_element_type=F32) + bglu_ref[...]
    s = s * jax.nn.sigmoid(z)
    y_b = jnp.dot((s * sg_ref[...]).astype(BF16), wssm_ref[...], preferred_element_type=F32)
    gates = gate_ref[...].astype(F32)
    mixed = gates[:, :D_MODEL] * y_a + gates[:, D_MODEL:] * y_b
    r = x_ref[...] + jnp.dot(mixed.astype(BF16), wout_ref[...], preferred_element_type=F32)
    y = r * lax.rsqrt(jnp.mean(r * r, axis=-1, keepdims=True) + EPS)
    o_ref[...] = (y * gfin_ref[...]).astype(o_ref.dtype)


def _out_stage(att, att_gate, y_ssm, u_gate, gates, x, d_skip, w_att_out, w_glu, b_glu, w_ssm_out, w_out,
               g_final, tm=256):
    l = x.shape[0]
    row = lambda n: pl.BlockSpec((tm, n), lambda i: (i, 0))
    return pl.pallas_call(
        _out_kernel,
        grid=(l // tm,),
        in_specs=[row(ATT_WIDTH), row(ATT_WIDTH), row(SSM_WIDTH), row(SSM_WIDTH),
                  pl.BlockSpec((tm, SSM_WIDTH), lambda i: (i, 1)),
                  row(N_BRANCH * D_MODEL), row(D_MODEL), _resident((1, SSM_WIDTH)),
                  _resident((ATT_WIDTH, D_MODEL)), _resident((SSM_WIDTH, SSM_WIDTH)), _resident((1, SSM_WIDTH)),
                  _resident((SSM_WIDTH, D_MODEL)), _resident((D_MODEL, D_MODEL)), _resident((1, D_MODEL))],
        out_specs=row(D_MODEL),
        out_shape=jax.ShapeDtypeStruct((l, D_MODEL), x.dtype),
        compiler_params=_params(("parallel",)),
        name="out_stage",
    )(att, att_gate, y_ssm, u_gate, u_gate, gates, x, d_skip.astype(F32).reshape(1, -1), w_att_out.astype(BF16),
      w_glu.astype(BF16), b_glu.reshape(1, -1), w_ssm_out.astype(BF16), w_out.astype(BF16), g_final.reshape(1, -1))


def _layer(x, positions, g_norm, w_in, b_merge, a_re, a_im, log_dt, ssm_b_re, ssm_b_im, ssm_c_re, ssm_c_im,
           ssm_d, w_glu, b_glu, w_att_out, w_ssm_out, w_out, g_out):
    off = [0]
    for s in SPLITS:
        off.append(off[-1] + s)
    head_end = off[7] + LANES - IDX_DIM - IDX_HEADS
    w_qkv = w_in[:, :off[3]].astype(BF16)
    w_mid = w_in[:, off[3]:head_end].astype(BF16)
    w_tail = w_in[:, off[7]:].astype(BF16)
    rope_att = _rope_tables(positions, HEAD_DIM)
    rope_idx = _rope_tables(positions, IDX_DIM)

    q_scale = HEAD_DIM ** -0.5 * LOG2E
    qkv_t, h = _project("qkv", x, w_qkv, 0, off[3],
                        ((off[0], ATT_WIDTH, "rope128", q_scale), (off[1], KV_WIDTH, "rope128", 1.0),
                         (off[2], KV_WIDTH, "none", 1.0)), BF16, rope_att, transpose_out=True, norm_gain=g_norm)
    q_t, kv_t = qkv_t, qkv_t[ATT_WIDTH:]
    att_gate, q_idx_t, misc_t = _project(
        "gate_idx", h, w_mid, 0, head_end - off[3],
        ((0, ATT_WIDTH, "silu", 1.0, 0), (off[4] - off[3], IDX_HEADS * IDX_DIM, "rope64", 1.0, 1),
         (off[5] - off[3], LANES, "idx_misc", IDX_HEADS ** -0.5 * IDX_DIM ** -0.5, 2)),
        (BF16, BF16, F32), rope_idx, transpose_out=(False, True, True))
    u_gate = _project("ssm", h, w_tail, 0, 2 * SSM_WIDTH,
                      ((0, SSM_WIDTH, "none", 1.0), (SSM_WIDTH, SSM_WIDTH, "silu", 1.0)), F32)
    gates = _project("merge", h, w_tail, 2 * SSM_WIDTH, N_BRANCH * D_MODEL, ((0, 1024, "sigmoid", 1.0),), BF16,
                     bias=b_merge, tm=1024, tn=1024)

    k_idx = misc_t[:IDX_DIM].T.astype(BF16)
    w_idx_t = misc_t[IDX_DIM:IDX_DIM + IDX_HEADS]
    att = _dsa(q_t, kv_t[:KV_WIDTH], kv_t[KV_WIDTH:], q_idx_t, k_idx, w_idx_t, BF16)
    y_ssm = _s5(u_gate[:, :SSM_WIDTH], a_re, a_im, log_dt, ssm_b_re, ssm_b_im, ssm_c_re, ssm_c_im)
    return _out_stage(att, att_gate, y_ssm, u_gate, gates, x, ssm_d, w_att_out, w_glu, b_glu, w_ssm_out, w_out, g_out)


def kernel(x, positions, g_norm, w_in, b_merge, a_re, a_im, log_dt, ssm_b_re, ssm_b_im, ssm_c_re, ssm_c_im,
           ssm_d, w_glu, b_glu, w_att_out, w_ssm_out, w_out, g_final):
    depth = w_in.shape[0]
    assert depth == 1 and x.shape[0] == 1, "single layer, single sequence"
    y = _layer(x[0], positions[0], g_norm[0], w_in[0], b_merge[0], a_re[0], a_im[0], log_dt[0], ssm_b_re[0],
               ssm_b_im[0], ssm_c_re[0], ssm_c_im[0], ssm_d[0], w_glu[0], b_glu[0], w_att_out[0], w_ssm_out[0],
               w_out[0], g_final)
    return y[None]
```

```python
import functools
import math

import jax
import jax.numpy as jnp
from jax import lax
from jax.experimental import pallas as pl
from jax.experimental.pallas import tpu as pltpu

D_MODEL = 2048
CHUNK = 64
EPS = 1e-6
ATT_HEADS = 8
ATT_KV_HEADS = 2
HEAD_DIM = 128
ATT_WIDTH = ATT_HEADS * HEAD_DIM
KV_WIDTH = ATT_KV_HEADS * HEAD_DIM
IDX_HEADS = 16
IDX_DIM = 64
INDEX_TOPK = 256
ROPE_THETA = 500000.0
ROPE_FRACTION = 4
SSM_GROUP_CH = 16
SSM_GROUPS = 32
SSM_WIDTH = SSM_GROUPS * SSM_GROUP_CH
SSM_STATE = 64
N_BRANCH = 2
SPLITS = (ATT_WIDTH, KV_WIDTH, KV_WIDTH, ATT_WIDTH, IDX_HEADS * IDX_DIM, IDX_DIM, IDX_HEADS,
          SSM_WIDTH, SSM_WIDTH, N_BRANCH * D_MODEL)

LANES = 128
SUBLANES = 8
VMEM_LIMIT = 56 * 1024 * 1024

Q_TILE = 256
KEY_TILE = 256
IDX_TILE = 512
ATT_CHUNK = 512
V_ROWS = HEAD_DIM + 16
BLIND_PASSES = 18
SSM_T = 64
LOG2E = 1.4426950408889634
NEG_BIG = -1e30

F32 = jnp.float32
BF16 = jnp.bfloat16


def _params(semantics):
    return pltpu.CompilerParams(dimension_semantics=semantics, vmem_limit_bytes=VMEM_LIMIT)


def _resident(shape):
    zeros = (0,) * len(shape)
    return pl.BlockSpec(shape, lambda *_: zeros, pipeline_mode=pl.Buffered(1))


def _rope_kernel(pos_ref, freq_ref, cos_ref, sin_ref):
    ang = freq_ref[...] * pos_ref[...].astype(F32)
    cos_ref[...] = jnp.cos(ang)
    sin_ref[...] = jnp.sin(ang)


def _rope_tables(positions, head_dim):
    l = positions.shape[0]
    rd = head_dim // ROPE_FRACTION
    half = rd // 2
    inv_freq = jnp.exp(-math.log(ROPE_THETA) * jnp.arange(half, dtype=F32) * (2.0 / rd))
    return pl.pallas_call(
        _rope_kernel,
        out_shape=(jax.ShapeDtypeStruct((half, l), F32),) * 2,
        name="rope_tables_%d" % head_dim,
    )(positions.reshape(1, l), inv_freq.reshape(half, 1))


def _silu(x):
    return x * jax.nn.sigmoid(x)


def _rotate_rows(xt, cos, sin, head_dim):
    half = head_dim // ROPE_FRACTION // 2
    pieces = []
    for base in range(0, LANES, head_dim):
        top, bot = xt[base:base + half], xt[base + half:base + 2 * half]
        pieces += [top * cos - bot * sin, bot * cos + top * sin, xt[base + 2 * half:base + head_dim]]
    return jnp.concatenate(pieces, axis=0)


def _proj_kernel(*refs, segments, has_rope, has_bias, has_norm, transposed):
    h_ref, w_ref = refs[0], refs[1]
    pos = 2
    if has_norm:
        gain_ref = refs[pos]
        pos += 1
    if has_bias:
        bias_ref = refs[pos]
        pos += 1
    if has_rope:
        cos, sin = refs[pos][...], refs[pos + 1][...]
        pos += 2
    o_refs = refs[pos:pos + len(transposed)]
    if has_norm:
        x = h_ref[...]
        h = (x * lax.rsqrt(jnp.mean(x * x, axis=-1, keepdims=True) + EPS) * gain_ref[...]).astype(BF16)
        refs[pos + len(transposed)][...] = h
    else:
        h = h_ref[...]
    acc = jnp.dot(h, w_ref[...], preferred_element_type=F32)
    if has_bias:
        acc = acc + bias_ref[...]
    out_pos = [0] * len(transposed)
    for start, size, kind, scale, k in segments:
        o_ref = o_refs[k]
        for b in range(size // LANES):
            xb = acc[:, start + b * LANES:start + (b + 1) * LANES]
            lo = out_pos[k]
            out_pos[k] += LANES
            if kind == "silu":
                xb = _silu(xb)
            elif kind == "sigmoid":
                xb = jax.nn.sigmoid(xb)
            if not transposed[k]:
                assert kind in ("none", "silu", "sigmoid") and scale == 1.0
                o_ref[:, lo:lo + LANES] = xb.astype(o_ref.dtype)
                continue
            xt = xb.T
            if kind == "rope128":
                xt = _rotate_rows(xt, cos, sin, HEAD_DIM)
            elif kind == "rope64":
                xt = _rotate_rows(xt, cos, sin, IDX_DIM)
            elif kind == "idx_misc":
                xt = jnp.concatenate([_rotate_rows(xt, cos, sin, IDX_DIM)[:IDX_DIM], xt[IDX_DIM:] * scale], axis=0)
            if kind != "idx_misc" and scale != 1.0:
                xt = xt * scale
            o_ref[lo:lo + LANES, :] = xt.astype(o_ref.dtype)


def _project(name, h, w, col0, n, segments, out_dtype, rope=None, bias=None, tm=512, tn=None,
             transpose_out=False, norm_gain=None):
    l, d = h.shape
    tn = n if tn is None else tn
    assert col0 % tn == 0 and n % tn == 0
    assert norm_gain is None or n == tn
    jb = col0 // tn
    single = not isinstance(out_dtype, tuple)
    dtypes = (out_dtype,) if single else out_dtype
    transposed = (transpose_out,) if single else transpose_out
    segments = tuple(seg if len(seg) == 5 else seg + (0,) for seg in segments)
    assert single or n == tn
    out_spec, out_shape = [], []
    for k, (dt, tr) in enumerate(zip(dtypes, transposed)):
        wk = tn if single else sum(seg[1] for seg in segments if seg[4] == k)
        nk = n if single else wk
        if tr:
            out_spec.append(pl.BlockSpec((wk, tm), lambda i, j: (j, i)))
            out_shape.append(jax.ShapeDtypeStruct((nk, l), dt))
        else:
            out_spec.append(pl.BlockSpec((tm, wk), lambda i, j: (i, j)))
            out_shape.append(jax.ShapeDtypeStruct((l, nk), dt))
    in_specs = [pl.BlockSpec((tm, d), lambda i, j: (i, 0)), pl.BlockSpec((d, tn), lambda i, j: (0, jb + j))]
    args = [h, w]
    if norm_gain is not None:
        in_specs.append(pl.BlockSpec((1, d), lambda i, j: (0, 0)))
        args.append(norm_gain.reshape(1, d))
        out_spec.append(pl.BlockSpec((tm, d), lambda i, j: (i, 0)))
        out_shape.append(jax.ShapeDtypeStruct((l, d), BF16))
    out_spec, out_shape = (out_spec[0], out_shape[0]) if len(out_spec) == 1 else (tuple(out_spec), tuple(out_shape))
    if bias is not None:
        in_specs.append(pl.BlockSpec((1, tn), lambda i, j: (0, j)))
        args.append(bias.reshape(1, n))
    if rope is not None:
        in_specs += [pl.BlockSpec((rope[0].shape[0], tm), lambda i, j: (0, i))] * 2
        args += list(rope)
    kern = functools.partial(_proj_kernel, segments=segments, has_rope=rope is not None, has_bias=bias is not None,
                             has_norm=norm_gain is not None, transposed=transposed)
    return pl.pallas_call(
        kern,
        grid=(l // tm, n // tn),
        in_specs=in_specs,
        out_specs=out_spec,
        out_shape=out_shape,
        compiler_params=_params(("parallel", "arbitrary")),
        name="in_proj_" + name,
    )(*args)


def _col_reduce(x, op):
    r, q = x.shape
    y = op(x.reshape(r // 64, 64, q), axis=0) if r > 64 else x
    y = op(y.reshape(8, 8, q), axis=0)
    return op(y, axis=0, keepdims=True)


def _dsa_kernel(qt_ref, qit_ref, wt_ref, kidx_ref, k_ref, vt_ref, o_ref,
                sc_ref, s_ref, acc_ref, *, topk, n_att_chunks):
    i = pl.program_id(0)
    n_blk = i + 1
    sub = SUBLANES
    grp = ATT_HEADS // ATT_KV_HEADS
    inf = float("inf")

    def score_block(t, carry, masked):
        smin, smax = carry
        start = pl.multiple_of(t * IDX_TILE, IDX_TILE)
        rows = pl.ds(start, IDX_TILE)
        kt = kidx_ref[rows, :]
        acc = jnp.zeros((IDX_TILE, Q_TILE), F32)
        for h in range(IDX_HEADS):
            logit = jnp.dot(kt, qit_ref[h * IDX_DIM:(h + 1) * IDX_DIM, :], preferred_element_type=F32)
            acc = acc + wt_ref[h:h + 1, :] * jnp.maximum(logit, 0.0)
        lo_src = acc
        if masked:
            key_chunk = (start + lax.broadcasted_iota(jnp.int32, acc.shape, 0)) // CHUNK
            q_chunk = i * (Q_TILE // CHUNK) + lax.broadcasted_iota(jnp.int32, acc.shape, 1) // CHUNK
            lo_src = jnp.where(key_chunk <= q_chunk, acc, inf)
            acc = jnp.where(key_chunk <= q_chunk, acc, -inf)
        sc_ref[rows, :] = acc
        return (jnp.minimum(smin, jnp.min(lo_src.reshape(IDX_TILE // sub, sub, Q_TILE), axis=0)),
                jnp.maximum(smax, jnp.max(acc.reshape(IDX_TILE // sub, sub, Q_TILE), axis=0)))

    n_full = (i * Q_TILE) // IDX_TILE
    carry = lax.fori_loop(0, n_full, lambda t, c: score_block(t, c, False),
                          (jnp.full((sub, Q_TILE), inf, F32), jnp.full((sub, Q_TILE), -inf, F32)))
    smin, smax = score_block(n_full, carry, True)
    smin = jnp.min(smin, axis=0, keepdims=True)
    smax = jnp.max(smax, axis=0, keepdims=True)

    blk_per_it = 2 * ATT_CHUNK // KEY_TILE
    n_it = (n_blk + blk_per_it - 1) // blk_per_it

    def fill_body(t, carry):
        sc_ref[pl.ds(pl.multiple_of(t * KEY_TILE, KEY_TILE), KEY_TILE), :] = jnp.full(
            (KEY_TILE, Q_TILE), -inf, F32)
        return carry

    lax.fori_loop(n_blk, n_it * blk_per_it, fill_body, 0)

    def count_ge(cand):
        def count_rows(start, rows, accs):
            blk = sc_ref[pl.ds(pl.multiple_of(start, KEY_TILE), rows), :]
            hit = (blk >= cand).astype(jnp.int32)
            return accs + jnp.sum(hit.reshape(4, rows // (4 * sub), sub, Q_TILE), axis=1)

        accs = lax.fori_loop(0, n_blk // 2, lambda t, a: count_rows(t * 2 * KEY_TILE, 2 * KEY_TILE, a),
                             jnp.zeros((4, sub, Q_TILE), jnp.int32))
        accs = lax.fori_loop(2 * (n_blk // 2), n_blk, lambda t, a: count_rows(t * KEY_TILE, KEY_TILE, a), accs)
        return jnp.sum(jnp.sum(accs, axis=0), axis=0, keepdims=True)

    lane_chunk = lax.broadcasted_iota(jnp.int32, (1, Q_TILE), 1) // CHUNK
    n_adm = (i * (Q_TILE // CHUNK) + lane_chunk + 1) * CHUNK
    hi0 = smax + (jnp.abs(smax) * 1e-6 + 1e-30)

    def bis_step(state):
        lo, hi, cnt_lo, cnt_hi, live_i = state
        mid = lo + 0.5 * (hi - lo)
        live = (live_i > 0) & (mid > lo) & (mid < hi)
        cnt = count_ge(mid)
        ge = cnt >= topk
        lo = jnp.where(live & ge, mid, lo)
        hi = jnp.where(live & ~ge, mid, hi)
        cnt_lo = jnp.where(live & ge, cnt, cnt_lo)
        cnt_hi = jnp.where(live & ~ge, cnt, cnt_hi)
        return lo, hi, cnt_lo, cnt_hi, (live & (cnt_lo > topk)).astype(jnp.int32)

    def bis_body(carry):
        state = bis_step(carry[1:])
        return (jnp.max(state[4]),) + state

    state = lax.fori_loop(0, BLIND_PASSES, lambda _, st: bis_step(st),
                          (smin, hi0, n_adm, jnp.zeros((1, Q_TILE), jnp.int32), (n_adm > topk).astype(jnp.int32)))
    _, thr, _, cnt_lo, cnt_hi, _ = lax.while_loop(lambda c: c[0] > 0, bis_body, (jnp.max(state[4]),) + state)

    tied = cnt_lo > topk

    @pl.when(jnp.max(tied.astype(jnp.int32)) > 0)
    def _():
        keep = (topk - cnt_hi).astype(F32)
        tri = (lax.broadcasted_iota(jnp.int32, (KEY_TILE, KEY_TILE), 0)
               >= lax.broadcasted_iota(jnp.int32, (KEY_TILE, KEY_TILE), 1)).astype(BF16)

        def fix_body(t, before):
            rows = pl.ds(pl.multiple_of(t * KEY_TILE, KEY_TILE), KEY_TILE)
            blk = sc_ref[rows, :]
            eq = (blk == thr) & tied
            eq_f = eq.astype(F32)
            rank = jnp.dot(tri, eq_f.astype(BF16), preferred_element_type=F32) - 1.0 + before
            sc_ref[rows, :] = jnp.where(eq & (rank >= keep), -inf, blk)
            return before + jnp.sum(eq_f, axis=0, keepdims=True)

        lax.fori_loop(0, n_blk, fix_body, jnp.zeros((1, Q_TILE), F32))

    last_chunk = n_att_chunks - 1

    def qk(h, c):
        rows = pl.ds(pl.multiple_of(c * ATT_CHUNK, ATT_CHUNK), ATT_CHUNK)
        return jnp.dot(k_ref[h // grp, rows, :], qt_ref[h * HEAD_DIM:(h + 1) * HEAD_DIM, :],
                       preferred_element_type=F32)

    for kv in range(ATT_KV_HEADS):
        heads = range(kv * grp, (kv + 1) * grp)
        acc_ref[...] = jnp.zeros(acc_ref.shape, F32)
        for g, h in enumerate(heads):
            s_ref[0, g] = qk(h, 0)

        def att_body(it, carry, kv=kv, heads=heads):
            ms, ls = list(carry[0]), list(carry[1])
            for si in range(2):
                c = it * 2 + si
                rows = pl.ds(pl.multiple_of(c * ATT_CHUNK, ATT_CHUNK), ATT_CHUNK)
                bias = jnp.where(sc_ref[rows, :] >= thr, 0.0, NEG_BIG)
                v_t = vt_ref[c, kv * V_ROWS:(kv + 1) * V_ROWS, :]
                for g, h in enumerate(heads):
                    s = s_ref[si, g] + bias
                    s_ref[1 - si, g] = qk(h, jnp.minimum(c + 1, last_chunk))
                    m_new = jnp.maximum(ms[g], _col_reduce(s, jnp.max))
                    alpha = jnp.exp2(ms[g] - m_new)
                    p = jnp.exp2(s - m_new)
                    pv = jnp.dot(v_t, p.astype(BF16), preferred_element_type=F32)
                    ls[g] = alpha * ls[g] + pv[HEAD_DIM:HEAD_DIM + 1]
                    acc_ref[g] = alpha * acc_ref[g] + pv[:HEAD_DIM]
                    ms[g] = m_new
            return tuple(ms), tuple(ls)

        m0 = tuple(jnp.full((1, Q_TILE), NEG_BIG, F32) for _ in heads)
        l0 = tuple(jnp.zeros((1, Q_TILE), F32) for _ in heads)
        _, ls = lax.fori_loop(0, n_it, att_body, (m0, l0))
        for g, h in enumerate(heads):
            o_ref[:, h * HEAD_DIM:(h + 1) * HEAD_DIM] = (acc_ref[g] / ls[g]).T.astype(o_ref.dtype)


def _dsa(q_t, k_t, v_t, q_idx_t, k_idx, w_idx_t, out_dtype):
    l = q_t.shape[1]
    topk = min(INDEX_TOPK, l // 4)
    assert l % (2 * ATT_CHUNK) == 0
    n_att_chunks = l // ATT_CHUNK
    k3 = k_t.reshape(ATT_KV_HEADS, HEAD_DIM, l).transpose(0, 2, 1)
    vt = v_t.reshape(ATT_KV_HEADS, HEAD_DIM, n_att_chunks, ATT_CHUNK)
    vt = jnp.concatenate([vt, jnp.ones((ATT_KV_HEADS, V_ROWS - HEAD_DIM, n_att_chunks, ATT_CHUNK), vt.dtype)], axis=1)
    vt = vt.reshape(ATT_KV_HEADS * V_ROWS, n_att_chunks, ATT_CHUNK).transpose(1, 0, 2)
    kern = functools.partial(_dsa_kernel, topk=topk, n_att_chunks=n_att_chunks)
    return pl.pallas_call(
        kern,
        grid=(l // Q_TILE,),
        in_specs=[
            pl.BlockSpec((ATT_WIDTH, Q_TILE), lambda i: (0, i)),
            pl.BlockSpec((IDX_HEADS * IDX_DIM, Q_TILE), lambda i: (0, i)),
            pl.BlockSpec((IDX_HEADS, Q_TILE), lambda i: (0, i)),
            _resident((l, IDX_DIM)),
            _resident((ATT_KV_HEADS, l, HEAD_DIM)),
            _resident((n_att_chunks, ATT_KV_HEADS * V_ROWS, ATT_CHUNK)),
        ],
        out_specs=pl.BlockSpec((Q_TILE, ATT_WIDTH), lambda i: (i, 0)),
        out_shape=jax.ShapeDtypeStruct((l, ATT_WIDTH), out_dtype),
        scratch_shapes=[
            pltpu.VMEM((l, Q_TILE), F32),
            pltpu.VMEM((2, ATT_HEADS // ATT_KV_HEADS, ATT_CHUNK, Q_TILE), F32),
            pltpu.VMEM((ATT_HEADS // ATT_KV_HEADS, HEAD_DIM, Q_TILE), F32),
        ],
        compiler_params=_params(("arbitrary",)),
        name="dsa",
    )(q_t, q_idx_t, w_idx_t, k_idx, k3, vt)


def _s5_kernel(x_ref, bcat_ref, gcat_ref, per_ref, pei_ref, qre_ref, qim_ref, lam_ref, y_ref,
               mt_ref, er_ref, ei_ref, sr_ref, si_ref):
    r0 = jnp.dot(bcat_ref[0], gcat_ref[0], precision=lax.Precision.HIGHEST, preferred_element_type=F32)
    lane = lax.broadcasted_iota(jnp.int32, r0.shape, 1)
    for s in range(SSM_T):
        shifted = r0 if s == 0 else pltpu.roll(r0, s * SSM_GROUP_CH, 1)
        mt_ref[s * SSM_GROUP_CH:(s + 1) * SSM_GROUP_CH, :] = jnp.where(
            lane >= s * SSM_GROUP_CH, shifted, 0.0).astype(BF16)
    xb = x_ref[0]
    er_ref[...] = jnp.dot(xb, per_ref[0], preferred_element_type=F32)
    ei_ref[...] = jnp.dot(xb, pei_ref[0], preferred_element_type=F32)
    lam_r = lam_ref[0, 0:1, :]
    lam_i = lam_ref[0, 1:2, :]
    n_chunks = xb.shape[0]

    def scan_body(c, carry):
        s_r, s_i = carry
        sr_ref[pl.ds(c, 1), :] = s_r
        si_ref[pl.ds(c, 1), :] = s_i
        e_r = er_ref[pl.ds(c, 1), :]
        e_i = ei_ref[pl.ds(c, 1), :]
        return (lam_r * s_r - lam_i * s_i + e_r, lam_r * s_i + lam_i * s_r + e_i)

    zero = jnp.zeros((1, SSM_STATE), F32)
    lax.fori_loop(0, n_chunks, scan_body, (zero, zero))
    y = jnp.dot(xb, mt_ref[...], preferred_element_type=F32)
    y = y + jnp.dot(sr_ref[...].astype(BF16), qre_ref[0], preferred_element_type=F32)
    y = y + jnp.dot(si_ref[...].astype(BF16), qim_ref[0], preferred_element_type=F32)
    y_ref[0] = y.astype(y_ref.dtype)


def _s5(u, a_re, a_im, log_dt, b_re, b_im, c_re, c_im):
    l = u.shape[0]
    t, c, p, g = SSM_T, SSM_GROUP_CH, SSM_STATE, SSM_GROUPS
    n_chunks = l // t
    width = t * c
    dt = jnp.exp(log_dt.astype(F32))[:, None]
    ar, ai = a_re.astype(F32), a_im.astype(F32)
    mag = jnp.exp(ar * dt)
    lb_re, lb_im = mag * jnp.cos(ai * dt), mag * jnp.sin(ai * dt)
    den = ar * ar + ai * ai
    coef_re = ((lb_re - 1.0) * ar + lb_im * ai) / den
    coef_im = (lb_im * ar - (lb_re - 1.0) * ai) / den
    br, bi = b_re.astype(F32), b_im.astype(F32)
    bb_re = coef_re[..., None] * br - coef_im[..., None] * bi
    bb_im = coef_re[..., None] * bi + coef_im[..., None] * br
    cr, ci = c_re.astype(F32), c_im.astype(F32)
    pr, pi = jnp.ones((g, 1, p), F32), jnp.zeros((g, 1, p), F32)
    sr, si = lb_re[:, None, :], lb_im[:, None, :]
    while pr.shape[1] < t + 1:
        pr, pi = (jnp.concatenate([pr, pr * sr - pi * si], axis=1),
                  jnp.concatenate([pi, pr * si + pi * sr], axis=1))
        sr, si = sr * sr - si * si, 2.0 * sr * si
    pr, pi = pr[:, :t + 1], pi[:, :t + 1]
    cr_t, ci_t = cr.transpose(0, 2, 1)[:, :, None, :], ci.transpose(0, 2, 1)[:, :, None, :]
    pr_t, pi_t = pr.transpose(0, 2, 1)[:, :, :, None], pi.transpose(0, 2, 1)[:, :, :, None]
    gc_re = cr_t * pr_t - ci_t * pi_t
    gc_im = cr_t * pi_t + ci_t * pr_t
    tb_re, tb_im = bb_re.transpose(0, 2, 1), bb_im.transpose(0, 2, 1)
    bcat = jnp.concatenate([tb_re, -tb_im], axis=2)
    gcat = jnp.concatenate([gc_re[:, :, :t], gc_im[:, :, :t]], axis=1).reshape(g, 2 * p, width)
    rr, ri = pr[:, :t][:, ::-1][:, :, None, :], pi[:, :t][:, ::-1][:, :, None, :]
    pe_re = (rr * tb_re[:, None] - ri * tb_im[:, None]).reshape(g, width, p)
    pe_im = (rr * tb_im[:, None] + ri * tb_re[:, None]).reshape(g, width, p)
    q_re = gc_re[:, :, 1:t + 1].reshape(g, p, width)
    q_im = gc_im[:, :, 1:t + 1].reshape(g, p, width)
    lam_pack = jnp.stack([pr[:, t], pi[:, t]], axis=1)
    x = u.astype(BF16).reshape(n_chunks, t, g, c).transpose(2, 0, 1, 3).reshape(g, n_chunks, width)
    grp3 = lambda a, b: pl.BlockSpec((1, a, b), lambda i: (i, 0, 0))
    y = pl.pallas_call(
        _s5_kernel,
        grid=(g,),
        in_specs=[grp3(n_chunks, width), grp3(c, 2 * p), grp3(2 * p, width), grp3(width, p), grp3(width, p),
                  grp3(p, width), grp3(p, width), grp3(2, p)],
        out_specs=grp3(n_chunks, width),
        out_shape=jax.ShapeDtypeStruct((g, n_chunks, width), BF16),
        scratch_shapes=[
            pltpu.VMEM((width, width), BF16),
            pltpu.VMEM((n_chunks, p), F32),
            pltpu.VMEM((n_chunks, p), F32),
            pltpu.VMEM((n_chunks, p), F32),
            pltpu.VMEM((n_chunks, p), F32),
        ],
        compiler_params=_params(("parallel",)),
        name="s5",
    )(x, bcat, gcat, pe_re.astype(BF16), pe_im.astype(BF16), q_re.astype(BF16), (-q_im).astype(BF16), lam_pack)
    return y.reshape(g, n_chunks, t, c).transpose(1, 2, 0, 3).reshape(l, SSM_WIDTH)


def _gelu_tanh(x):
    return 0.5 * x * (1.0 + jnp.tanh(math.sqrt(2.0 / math.pi) * (x + 0.044715 * (x * x * x))))


def _out_kernel(att_ref, ag_ref, ys_ref, u_ref, sg_ref, gate_ref, x_ref, d_ref, watt_ref, wglu_ref, bglu_ref,
                wssm_ref, wout_ref, gfin_ref, o_ref):
    a = (att_ref[...].astype(F32) * ag_ref[...].astype(F32)).astype(BF16)
    y_a = jnp.dot(a, watt_ref[...], preferred_element_type=F32)
    s = _gelu_tanh(ys_ref[...].astype(F32) + d_ref[...] * u_ref[...])
    z = jnp.dot(s.astype(BF16), wglu_ref[...], preferred_element_type=F32) + bglu_ref[...]
    s = s * jax.nn.sigmoid(z)
    y_b = jnp.dot((s * sg_ref[...]).astype(BF16), wssm_ref[...], preferred_element_type=F32)
    gates = gate_ref[...].astype(F32)
    mixed = gates[:, :D_MODEL] * y_a + gates[:, D_MODEL:] * y_b
    r = x_ref[...] + jnp.dot(mixed.astype(BF16), wout_ref[...], preferred_element_type=F32)
    y = r * lax.rsqrt(jnp.mean(r * r, axis=-1, keepdims=True) + EPS)
    o_ref[...] = (y * gfin_ref[...]).astype(o_ref.dtype)


def _out_stage(att, att_gate, y_ssm, u_gate, gates, x, d_skip, w_att_out, w_glu, b_glu, w_ssm_out, w_out,
               g_final, tm=256):
    l = x.shape[0]
    row = lambda n: pl.BlockSpec((tm, n), lambda i: (i, 0))
    return pl.pallas_call(
        _out_kernel,
        grid=(l // tm,),
        in_specs=[row(ATT_WIDTH), row(ATT_WIDTH), row(SSM_WIDTH), row(SSM_WIDTH),
                  pl.BlockSpec((tm, SSM_WIDTH), lambda i: (i, 1)),
                  row(N_BRANCH * D_MODEL), row(D_MODEL), _resident((1, SSM_WIDTH)),
                  _resident((ATT_WIDTH, D_MODEL)), _resident((SSM_WIDTH, SSM_WIDTH)), _resident((1, SSM_WIDTH)),
                  _resident((SSM_WIDTH, D_MODEL)), _resident((D_MODEL, D_MODEL)), _resident((1, D_MODEL))],
        out_specs=row(D_MODEL),
        out_shape=jax.ShapeDtypeStruct((l, D_MODEL), x.dtype),
        compiler_params=_params(("parallel",)),
        name="out_stage",
    )(att, att_gate, y_ssm, u_gate, u_gate, gates, x, d_skip.astype(F32).reshape(1, -1), w_att_out.astype(BF16),
      w_glu.astype(BF16), b_glu.reshape(1, -1), w_ssm_out.astype(BF16), w_out.astype(BF16), g_final.reshape(1, -1))


def _layer(x, positions, g_norm, w_in, b_merge, a_re, a_im, log_dt, ssm_b_re, ssm_b_im, ssm_c_re, ssm_c_im,
           ssm_d, w_glu, b_glu, w_att_out, w_ssm_out, w_out, g_out):
    off = [0]
    for s in SPLITS:
        off.append(off[-1] + s)
    head_end = off[7] + LANES - IDX_DIM - IDX_HEADS
    w_qkv = w_in[:, :off[3]].astype(BF16)
    w_mid = w_in[:, off[3]:head_end].astype(BF16)
    w_tail = w_in[:, off[7]:].astype(BF16)
    rope_att = _rope_tables(positions, HEAD_DIM)
    rope_idx = _rope_tables(positions, IDX_DIM)

    q_scale = HEAD_DIM ** -0.5 * LOG2E
    qkv_t, h = _project("qkv", x, w_qkv, 0, off[3],
                        ((off[0], ATT_WIDTH, "rope128", q_scale), (off[1], KV_WIDTH, "rope128", 1.0),
                         (off[2], KV_WIDTH, "none", 1.0)), BF16, rope_att, transpose_out=True, norm_gain=g_norm)
    q_t, kv_t = qkv_t, qkv_t[ATT_WIDTH:]
    att_gate, q_idx_t, misc_t = _project(
        "gate_idx", h, w_mid, 0, head_end - off[3],
        ((0, ATT_WIDTH, "silu", 1.0, 0), (off[4] - off[3], IDX_HEADS * IDX_DIM, "rope64", 1.0, 1),
         (off[5] - off[3], LANES, "idx_misc", IDX_HEADS ** -0.5 * IDX_DIM ** -0.5, 2)),
        (BF16, BF16, F32), rope_idx, transpose_out=(False, True, True))
    u_gate = _project("ssm", h, w_tail, 0, 2 * SSM_WIDTH,
                      ((0, SSM_WIDTH, "none", 1.0), (SSM_WIDTH, SSM_WIDTH, "silu", 1.0)), F32)
    gates = _project("merge", h, w_tail, 2 * SSM_WIDTH, N_BRANCH * D_MODEL, ((0, 1024, "sigmoid", 1.0),), BF16,
                     bias=b_merge, tm=1024, tn=1024)

    k_idx = misc_t[:IDX_DIM].T.astype(BF16)
    w_idx_t = misc_t[IDX_DIM:IDX_DIM + IDX_HEADS]
    att = _dsa(q_t, kv_t[:KV_WIDTH], kv_t[KV_WIDTH:], q_idx_t, k_idx, w_idx_t, BF16)
    y_ssm = _s5(u_gate[:, :SSM_WIDTH], a_re, a_im, log_dt, ssm_b_re, ssm_b_im, ssm_c_re, ssm_c_im)
    return _out_stage(att, att_gate, y_ssm, u_gate, gates, x, ssm_d, w_att_out, w_glu, b_glu, w_ssm_out, w_out, g_out)


def kernel(x, positions, g_norm, w_in, b_merge, a_re, a_im, log_dt, ssm_b_re, ssm_b_im, ssm_c_re, ssm_c_im,
           ssm_d, w_glu, b_glu, w_att_out, w_ssm_out, w_out, g_final):
    depth = w_in.shape[0]
    assert depth == 1 and x.shape[0] == 1, "single layer, single sequence"
    y = _layer(x[0], positions[0], g_norm[0], w_in[0], b_merge[0], a_re[0], a_im[0], log_dt[0], ssm_b_re[0],
               ssm_b_im[0], ssm_c_re[0], ssm_c_im[0], ssm_d[0], w_glu[0], b_glu[0], w_att_out[0], w_ssm_out[0],
               w_out[0], g_final)
    return y[None]
```

```python
import functools
import math

import jax
import jax.numpy as jnp
from jax import lax
from jax.experimental import pallas as pl
from jax.experimental.pallas import tpu as pltpu

D_MODEL = 2048
CHUNK = 64
EPS = 1e-6
ATT_HEADS = 8
ATT_KV_HEADS = 2
HEAD_DIM = 128
ATT_WIDTH = ATT_HEADS * HEAD_DIM
KV_WIDTH = ATT_KV_HEADS * HEAD_DIM
IDX_HEADS = 16
IDX_DIM = 64
INDEX_TOPK = 256
ROPE_THETA = 500000.0
ROPE_FRACTION = 4
SSM_GROUP_CH = 16
SSM_GROUPS = 32
SSM_WIDTH = SSM_GROUPS * SSM_GROUP_CH
SSM_STATE = 64
N_BRANCH = 2
SPLITS = (ATT_WIDTH, KV_WIDTH, KV_WIDTH, ATT_WIDTH, IDX_HEADS * IDX_DIM, IDX_DIM, IDX_HEADS,
          SSM_WIDTH, SSM_WIDTH, N_BRANCH * D_MODEL)

LANES = 128
SUBLANES = 8
VMEM_LIMIT = 56 * 1024 * 1024

Q_TILE = 256
KEY_TILE = 256
IDX_TILE = 512
ATT_CHUNK = 512
V_ROWS = HEAD_DIM + 16
BLIND_PASSES = 18
SSM_T = 64
LOG2E = 1.4426950408889634
NEG_BIG = -1e30

F32 = jnp.float32
BF16 = jnp.bfloat16


def _params(semantics):
    return pltpu.CompilerParams(dimension_semantics=semantics, vmem_limit_bytes=VMEM_LIMIT)


def _resident(shape):
    zeros = (0,) * len(shape)
    return pl.BlockSpec(shape, lambda *_: zeros, pipeline_mode=pl.Buffered(1))


def _rope_kernel(pos_ref, freq_ref, cos_ref, sin_ref):
    ang = freq_ref[...] * pos_ref[...].astype(F32)
    cos_ref[...] = jnp.cos(ang)
    sin_ref[...] = jnp.sin(ang)


def _rope_tables(positions, head_dim):
    l = positions.shape[0]
    rd = head_dim // ROPE_FRACTION
    half = rd // 2
    inv_freq = jnp.exp(-math.log(ROPE_THETA) * jnp.arange(half, dtype=F32) * (2.0 / rd))
    return pl.pallas_call(
        _rope_kernel,
        out_shape=(jax.ShapeDtypeStruct((half, l), F32),) * 2,
        name="rope_tables_%d" % head_dim,
    )(positions.reshape(1, l), inv_freq.reshape(half, 1))


def _silu(x):
    return x * jax.nn.sigmoid(x)


def _rotate_rows(xt, cos, sin, head_dim):
    half = head_dim // ROPE_FRACTION // 2
    pieces = []
    for base in range(0, LANES, head_dim):
        top, bot = xt[base:base + half], xt[base + half:base + 2 * half]
        pieces += [top * cos - bot * sin, bot * cos + top * sin, xt[base + 2 * half:base + head_dim]]
    return jnp.concatenate(pieces, axis=0)


def _proj_kernel(*refs, segments, has_rope, has_bias, has_norm, transposed):
    h_ref, w_ref = refs[0], refs[1]
    pos = 2
    if has_norm:
        gain_ref = refs[pos]
        pos += 1
    if has_bias:
        bias_ref = refs[pos]
        pos += 1
    if has_rope:
        cos, sin = refs[pos][...], refs[pos + 1][...]
        pos += 2
    o_refs = refs[pos:pos + len(transposed)]
    if has_norm:
        x = h_ref[...]
        h = (x * lax.rsqrt(jnp.mean(x * x, axis=-1, keepdims=True) + EPS) * gain_ref[...]).astype(BF16)
        refs[pos + len(transposed)][...] = h
    else:
        h = h_ref[...]
    acc = jnp.dot(h, w_ref[...], preferred_element_type=F32)
    if has_bias:
        acc = acc + bias_ref[...]
    out_pos = [0] * len(transposed)
    for start, size, kind, scale, k in segments:
        o_ref = o_refs[k]
        for b in range(size // LANES):
            xb = acc[:, start + b * LANES:start + (b + 1) * LANES]
            lo = out_pos[k]
            out_pos[k] += LANES
            if kind == "silu":
                xb = _silu(xb)
            elif kind == "sigmoid":
                xb = jax.nn.sigmoid(xb)
            if not transposed[k]:
                assert kind in ("none", "silu", "sigmoid") and scale == 1.0
                o_ref[:, lo:lo + LANES] = xb.astype(o_ref.dtype)
                continue
            xt = xb.T
            if kind == "rope128":
                xt = _rotate_rows(xt, cos, sin, HEAD_DIM)
            elif kind == "rope64":
                xt = _rotate_rows(xt, cos, sin, IDX_DIM)
            elif kind == "idx_misc":
                xt = jnp.concatenate([_rotate_rows(xt, cos, sin, IDX_DIM)[:IDX_DIM], xt[IDX_DIM:] * scale], axis=0)
            if kind != "idx_misc" and scale != 1.0:
                xt = xt * scale
            o_ref[lo:lo + LANES, :] = xt.astype(o_ref.dtype)


def _project(name, h, w, col0, n, segments, out_dtype, rope=None, bias=None, tm=512, tn=None,
             transpose_out=False, norm_gain=None):
    l, d = h.shape
    tn = n if tn is None else tn
    assert col0 % tn == 0 and n % tn == 0
    assert norm_gain is None or n == tn
    jb = col0 // tn
    single = not isinstance(out_dtype, tuple)
    dtypes = (out_dtype,) if single else out_dtype
    transposed = (transpose_out,) if single else transpose_out
    segments = tuple(seg if len(seg) == 5 else seg + (0,) for seg in segments)
    assert single or n == tn
    out_spec, out_shape = [], []
    for k, (dt, tr) in enumerate(zip(dtypes, transposed)):
        wk = tn if single else sum(seg[1] for seg in segments if seg[4] == k)
        nk = n if single else wk
        if tr:
            out_spec.append(pl.BlockSpec((wk, tm), lambda i, j: (j, i)))
            out_shape.append(jax.ShapeDtypeStruct((nk, l), dt))
        else:
            out_spec.append(pl.BlockSpec((tm, wk), lambda i, j: (i, j)))
            out_shape.append(jax.ShapeDtypeStruct((l, nk), dt))
    in_specs = [pl.BlockSpec((tm, d), lambda i, j: (i, 0)), pl.BlockSpec((d, tn), lambda i, j: (0, jb + j))]
    args = [h, w]
    if norm_gain is not None:
        in_specs.append(pl.BlockSpec((1, d), lambda i, j: (0, 0)))
        args.append(norm_gain.reshape(1, d))
        out_spec.append(pl.BlockSpec((tm, d), lambda i, j: (i, 0)))
        out_shape.append(jax.ShapeDtypeStruct((l, d), BF16))
    out_spec, out_shape = (out_spec[0], out_shape[0]) if len(out_spec) == 1 else (tuple(out_spec), tuple(out_shape))
    if bias is not None:
        in_specs.append(pl.BlockSpec((1, tn), lambda i, j: (0, j)))
        args.append(bias.reshape(1, n))
    if rope is not None:
        in_specs += [pl.BlockSpec((rope[0].shape[0], tm), lambda i, j: (0, i))] * 2
        args += list(rope)
    kern = functools.partial(_proj_kernel, segments=segments, has_rope=rope is not None, has_bias=bias is not None,
                             has_norm=norm_gain is not None, transposed=transposed)
    return pl.pallas_call(
        kern,
        grid=(l // tm, n // tn),
        in_specs=in_specs,
        out_specs=out_spec,
        out_shape=out_shape,
        compiler_params=_params(("parallel", "arbitrary")),
        name="in_proj_" + name,
    )(*args)


def _col_reduce(x, op):
    r, q = x.shape
    y = op(x.reshape(r // 64, 64, q), axis=0) if r > 64 else x
    y = op(y.reshape(8, 8, q), axis=0)
    return op(y, axis=0, keepdims=True)


def _dsa_kernel(qt_ref, qit_ref, wt_ref, kidx_ref, k_ref, vt_ref, o_ref,
                sc_ref, s_ref, acc_ref, *, topk, n_att_chunks):
    i = pl.program_id(0)
    n_blk = i + 1
    sub = SUBLANES
    grp = ATT_HEADS // ATT_KV_HEADS
    inf = float("inf")

    def score_block(t, carry, masked):
        smin, smax = carry
        start = pl.multiple_of(t * IDX_TILE, IDX_TILE)
        rows = pl.ds(start, IDX_TILE)
        kt = kidx_ref[rows, :]
        acc = jnp.zeros((IDX_TILE, Q_TILE), F32)
        for h in range(IDX_HEADS):
            logit = jnp.dot(kt, qit_ref[h * IDX_DIM:(h + 1) * IDX_DIM, :], preferred_element_type=F32)
            acc = acc + wt_ref[h:h + 1, :] * jnp.maximum(logit, 0.0)
        lo_src = acc
        if masked:
            key_chunk = (start + lax.broadcasted_iota(jnp.int32, acc.shape, 0)) // CHUNK
            q_chunk = i * (Q_TILE // CHUNK) + lax.broadcasted_iota(jnp.int32, acc.shape, 1) // CHUNK
            lo_src = jnp.where(key_chunk <= q_chunk, acc, inf)
            acc = jnp.where(key_chunk <= q_chunk, acc, -inf)
        sc_ref[rows, :] = acc
        return (jnp.minimum(smin, jnp.min(lo_src.reshape(IDX_TILE // sub, sub, Q_TILE), axis=0)),
                jnp.maximum(smax, jnp.max(acc.reshape(IDX_TILE // sub, sub, Q_TILE), axis=0)))

    n_full = (i * Q_TILE) // IDX_TILE
    carry = lax.fori_loop(0, n_full, lambda t, c: score_block(t, c, False),
                          (jnp.full((sub, Q_TILE), inf, F32), jnp.full((sub, Q_TILE), -inf, F32)))
    smin, smax = score_block(n_full, carry, True)
    smin = jnp.min(smin, axis=0, keepdims=True)
    smax = jnp.max(smax, axis=0, keepdims=True)

    blk_per_it = 2 * ATT_CHUNK // KEY_TILE
    n_it = (n_blk + blk_per_it - 1) // blk_per_it

    def fill_body(t, carry):
        sc_ref[pl.ds(pl.multiple_of(t * KEY_TILE, KEY_TILE), KEY_TILE), :] = jnp.full(
            (KEY_TILE, Q_TILE), -inf, F32)
        return carry

    lax.fori_loop(n_blk, n_it * blk_per_it, fill_body, 0)

    def count_ge(cand):
        def count_rows(start, rows, accs):
            blk = sc_ref[pl.ds(pl.multiple_of(start, KEY_TILE), rows), :]
            hit = (blk >= cand).astype(jnp.int32)
            return accs + jnp.sum(hit.reshape(4, rows // (4 * sub), sub, Q_TILE), axis=1)

        accs = lax.fori_loop(0, n_blk // 2, lambda t, a: count_rows(t * 2 * KEY_TILE, 2 * KEY_TILE, a),
                             jnp.zeros((4, sub, Q_TILE), jnp.int32))
        accs = lax.fori_loop(2 * (n_blk // 2), n_blk, lambda t, a: count_rows(t * KEY_TILE, KEY_TILE, a), accs)
        return jnp.sum(jnp.sum(accs, axis=0), axis=0, keepdims=True)

    lane_chunk = lax.broadcasted_iota(jnp.int32, (1, Q_TILE), 1) // CHUNK
    n_adm = (i * (Q_TILE // CHUNK) + lane_chunk + 1) * CHUNK
    hi0 = smax + (jnp.abs(smax) * 1e-6 + 1e-30)

    def bis_step(state):
        lo, hi, cnt_lo, cnt_hi, live_i = state
        mid = lo + 0.5 * (hi - lo)
        live = (live_i > 0) & (mid > lo) & (mid < hi)
        cnt = count_ge(mid)
        ge = cnt >= topk
        lo = jnp.where(live & ge, mid, lo)
        hi = jnp.where(live & ~ge, mid, hi)
        cnt_lo = jnp.where(live & ge, cnt, cnt_lo)
        cnt_hi = jnp.where(live & ~ge, cnt, cnt_hi)
        return lo, hi, cnt_lo, cnt_hi, (live & (cnt_lo > topk)).astype(jnp.int32)

    def bis_body(carry):
        state = bis_step(carry[1:])
        return (jnp.max(state[4]),) + state

    state = lax.fori_loop(0, BLIND_PASSES, lambda _, st: bis_step(st),
                          (smin, hi0, n_adm, jnp.zeros((1, Q_TILE), jnp.int32), (n_adm > topk).astype(jnp.int32)))
    _, thr, _, cnt_lo, cnt_hi, _ = lax.while_loop(lambda c: c[0] > 0, bis_body, (jnp.max(state[4]),) + state)

    tied = cnt_lo > topk

    @pl.when(jnp.max(tied.astype(jnp.int32)) > 0)
    def _():
        keep = (topk - cnt_hi).astype(F32)
        tri = (lax.broadcasted_iota(jnp.int32, (KEY_TILE, KEY_TILE), 0)
               >= lax.broadcasted_iota(jnp.int32, (KEY_TILE, KEY_TILE), 1)).astype(BF16)

        def fix_body(t, before):
            rows = pl.ds(pl.multiple_of(t * KEY_TILE, KEY_TILE), KEY_TILE)
            blk = sc_ref[rows, :]
            eq = (blk == thr) & tied
            eq_f = eq.astype(F32)
            rank = jnp.dot(tri, eq_f.astype(BF16), preferred_element_type=F32) - 1.0 + before
            sc_ref[rows, :] = jnp.where(eq & (rank >= keep), -inf, blk)
            return before + jnp.sum(eq_f, axis=0, keepdims=True)

        lax.fori_loop(0, n_blk, fix_body, jnp.zeros((1, Q_TILE), F32))

    last_chunk = n_att_chunks - 1

    def qk(h, c):
        rows = pl.ds(pl.multiple_of(c * ATT_CHUNK, ATT_CHUNK), ATT_CHUNK)
        return jnp.dot(k_ref[h // grp, rows, :], qt_ref[h * HEAD_DIM:(h + 1) * HEAD_DIM, :],
                       preferred_element_type=F32)

    for kv in range(ATT_KV_HEADS):
        heads = range(kv * grp, (kv + 1) * grp)
        acc_ref[...] = jnp.zeros(acc_ref.shape, F32)
        for g, h in enumerate(heads):
            s_ref[0, g] = qk(h, 0)

        def att_body(it, carry, kv=kv, heads=heads):
            ms, ls = list(carry[0]), list(carry[1])
            for si in range(2):
                c = it * 2 + si
                rows = pl.ds(pl.multiple_of(c * ATT_CHUNK, ATT_CHUNK), ATT_CHUNK)
                bias = jnp.where(sc_ref[rows, :] >= thr, 0.0, NEG_BIG)
                v_t = vt_ref[c, kv * V_ROWS:(kv + 1) * V_ROWS, :]
                for g, h in enumerate(heads):
                    s = s_ref[si, g] + bias
                    s_ref[1 - si, g] = qk(h, jnp.minimum(c + 1, last_chunk))
                    m_new = jnp.maximum(ms[g], _col_reduce(s, jnp.max))
                    alpha = jnp.exp2(ms[g] - m_new)
                    p = jnp.exp2(s - m_new)
                    pv = jnp.dot(v_t, p.astype(BF16), preferred_element_type=F32)
                    ls[g] = alpha * ls[g] + pv[HEAD_DIM:HEAD_DIM + 1]
                    acc_ref[g] = alpha * acc_ref[g] + pv[:HEAD_DIM]
                    ms[g] = m_new
            return tuple(ms), tuple(ls)

        m0 = tuple(jnp.full((1, Q_TILE), NEG_BIG, F32) for _ in heads)
        l0 = tuple(jnp.zeros((1, Q_TILE), F32) for _ in heads)
        _, ls = lax.fori_loop(0, n_it, att_body, (m0, l0))
        for g, h in enumerate(heads):
            o_ref[:, h * HEAD_DIM:(h + 1) * HEAD_DIM] = (acc_ref[g] / ls[g]).T.astype(o_ref.dtype)


def _dsa(q_t, k_t, v_t, q_idx_t, k_idx, w_idx_t, out_dtype):
    l = q_t.shape[1]
    topk = min(INDEX_TOPK, l // 4)
    assert l % (2 * ATT_CHUNK) == 0
    n_att_chunks = l // ATT_CHUNK
    k3 = k_t.reshape(ATT_KV_HEADS, HEAD_DIM, l).transpose(0, 2, 1)
    vt = v_t.reshape(ATT_KV_HEADS, HEAD_DIM, n_att_chunks, ATT_CHUNK)
    vt = jnp.concatenate([vt, jnp.ones((ATT_KV_HEADS, V_ROWS - HEAD_DIM, n_att_chunks, ATT_CHUNK), vt.dtype)], axis=1)
    vt = vt.reshape(ATT_KV_HEADS * V_ROWS, n_att_chunks, ATT_CHUNK).transpose(1, 0, 2)
    kern = functools.partial(_dsa_kernel, topk=topk, n_att_chunks=n_att_chunks)
    return pl.pallas_call(
        kern,
        grid=(l // Q_TILE,),
        in_specs=[
            pl.BlockSpec((ATT_WIDTH, Q_TILE), lambda i: (0, i)),
            pl.BlockSpec((IDX_HEADS * IDX_DIM, Q_TILE), lambda i: (0, i)),
            pl.BlockSpec((IDX_HEADS, Q_TILE), lambda i: (0, i)),
            _resident((l, IDX_DIM)),
            _resident((ATT_KV_HEADS, l, HEAD_DIM)),
            _resident((n_att_chunks, ATT_KV_HEADS * V_ROWS, ATT_CHUNK)),
        ],
        out_specs=pl.BlockSpec((Q_TILE, ATT_WIDTH), lambda i: (i, 0)),
        out_shape=jax.ShapeDtypeStruct((l, ATT_WIDTH), out_dtype),
        scratch_shapes=[
            pltpu.VMEM((l, Q_TILE), F32),
            pltpu.VMEM((2, ATT_HEADS // ATT_KV_HEADS, ATT_CHUNK, Q_TILE), F32),
            pltpu.VMEM((ATT_HEADS // ATT_KV_HEADS, HEAD_DIM, Q_TILE), F32),
        ],
        compiler_params=_params(("arbitrary",)),
        name="dsa",
    )(q_t, q_idx_t, w_idx_t, k_idx, k3, vt)


def _s5_kernel(x_ref, bcat_ref, gcat_ref, per_ref, pei_ref, qre_ref, qim_ref, lam_ref, y_ref,
               mt_ref, er_ref, ei_ref, sr_ref, si_ref):
    r0 = jnp.dot(bcat_ref[0], gcat_ref[0], precision=lax.Precision.HIGHEST, preferred_element_type=F32)
    lane = lax.broadcasted_iota(jnp.int32, r0.shape, 1)
    for s in range(SSM_T):
        shifted = r0 if s == 0 else pltpu.roll(r0, s * SSM_GROUP_CH, 1)
        mt_ref[s * SSM_GROUP_CH:(s + 1) * SSM_GROUP_CH, :] = jnp.where(
            lane >= s * SSM_GROUP_CH, shifted, 0.0).astype(BF16)
    xb = x_ref[0]
    er_ref[...] = jnp.dot(xb, per_ref[0], preferred_element_type=F32)
    ei_ref[...] = jnp.dot(xb, pei_ref[0], preferred_element_type=F32)
    lam_r = lam_ref[0, 0:1, :]
    lam_i = lam_ref[0, 1:2, :]
    n_chunks = xb.shape[0]

    def scan_body(c, carry):
        s_r, s_i = carry
        sr_ref[pl.ds(c, 1), :] = s_r
        si_ref[pl.ds(c, 1), :] = s_i
        e_r = er_ref[pl.ds(c, 1), :]
        e_i = ei_ref[pl.ds(c, 1), :]
        return (lam_r * s_r - lam_i * s_i + e_r, lam_r * s_i + lam_i * s_r + e_i)

    zero = jnp.zeros((1, SSM_STATE), F32)
    lax.fori_loop(0, n_chunks, scan_body, (zero, zero))
    y = jnp.dot(xb, mt_ref[...], preferred_element_type=F32)
    y = y + jnp.dot(sr_ref[...].astype(BF16), qre_ref[0], preferred_element_type=F32)
    y = y + jnp.dot(si_ref[...].astype(BF16), qim_ref[0], preferred_element_type=F32)
    y_ref[0] = y.astype(y_ref.dtype)


def _s5(u, a_re, a_im, log_dt, b_re, b_im, c_re, c_im):
    l = u.shape[0]
    t, c, p, g = SSM_T, SSM_GROUP_CH, SSM_STATE, SSM_GROUPS
    n_chunks = l // t
    width = t * c
    dt = jnp.exp(log_dt.astype(F32))[:, None]
    ar, ai = a_re.astype(F32), a_im.astype(F32)
    mag = jnp.exp(ar * dt)
    lb_re, lb_im = mag * jnp.cos(ai * dt), mag * jnp.sin(ai * dt)
    den = ar * ar + ai * ai
    coef_re = ((lb_re - 1.0) * ar + lb_im * ai) / den
    coef_im = (lb_im * ar - (lb_re - 1.0) * ai) / den
    br, bi = b_re.astype(F32), b_im.astype(F32)
    bb_re = coef_re[..., None] * br - coef_im[..., None] * bi
    bb_im = coef_re[..., None] * bi + coef_im[..., None] * br
    cr, ci = c_re.astype(F32), c_im.astype(F32)
    pr, pi = jnp.ones((g, 1, p), F32), jnp.zeros((g, 1, p), F32)
    sr, si = lb_re[:, None, :], lb_im[:, None, :]
    while pr.shape[1] < t + 1:
        pr, pi = (jnp.concatenate([pr, pr * sr - pi * si], axis=1),
                  jnp.concatenate([pi, pr * si + pi * sr], axis=1))
        sr, si = sr * sr - si * si, 2.0 * sr * si
    pr, pi = pr[:, :t + 1], pi[:, :t + 1]
    cr_t, ci_t = cr.transpose(0, 2, 1)[:, :, None, :], ci.transpose(0, 2, 1)[:, :, None, :]
    pr_t, pi_t = pr.transpose(0, 2, 1)[:, :, :, None], pi.transpose(0, 2, 1)[:, :, :, None]
    gc_re = cr_t * pr_t - ci_t * pi_t
    gc_im = cr_t * pi_t + ci_t * pr_t
    tb_re, tb_im = bb_re.transpose(0, 2, 1), bb_im.transpose(0, 2, 1)
    bcat = jnp.concatenate([tb_re, -tb_im], axis=2)
    gcat = jnp.concatenate([gc_re[:, :, :t], gc_im[:, :, :t]], axis=1).reshape(g, 2 * p, width)
    rr, ri = pr[:, :t][:, ::-1][:, :, None, :], pi[:, :t][:, ::-1][:, :, None, :]
    pe_re = (rr * tb_re[:, None] - ri * tb_im[:, None]).reshape(g, width, p)
    pe_im = (rr * tb_im[:, None] + ri * tb_re[:, None]).reshape(g, width, p)
    q_re = gc_re[:, :, 1:t + 1].reshape(g, p, width)
    q_im = gc_im[:, :, 1:t + 1].reshape(g, p, width)
    lam_pack = jnp.stack([pr[:, t], pi[:, t]], axis=1)
    x = u.astype(BF16).reshape(n_chunks, t, g, c).transpose(2, 0, 1, 3).reshape(g, n_chunks, width)
    grp3 = lambda a, b: pl.BlockSpec((1, a, b), lambda i: (i, 0, 0))
    y = pl.pallas_call(
        _s5_kernel,
        grid=(g,),
        in_specs=[grp3(n_chunks, width), grp3(c, 2 * p), grp3(2 * p, width), grp3(width, p), grp3(width, p),
                  grp3(p, width), grp3(p, width), grp3(2, p)],
        out_specs=grp3(n_chunks, width),
        out_shape=jax.ShapeDtypeStruct((g, n_chunks, width), BF16),
        scratch_shapes=[
            pltpu.VMEM((width, width), BF16),
            pltpu.VMEM((n_chunks, p), F32),
            pltpu.VMEM((n_chunks, p), F32),
            pltpu.VMEM((n_chunks, p), F32),
            pltpu.VMEM((n_chunks, p), F32),
        ],
        compiler_params=_params(("parallel",)),
        name="s5",
    )(x, bcat, gcat, pe_re.astype(BF16), pe_im.astype(BF16), q_re.astype(BF16), (-q_im).astype(BF16), lam_pack)
    return y.reshape(g, n_chunks, t, c).transpose(1, 2, 0, 3).reshape(l, SSM_WIDTH)


def _gelu_tanh(x):
    return 0.5 * x * (1.0 + jnp.tanh(math.sqrt(2.0 / math.pi) * (x + 0.044715 * (x * x * x))))


def _out_kernel(att_ref, ag_ref, ys_ref, u_ref, sg_ref, gate_ref, x_ref, d_ref, watt_ref, wglu_ref, bglu_ref,
                wssm_ref, wout_ref, gfin_ref, o_ref):
    a = (att_ref[...].astype(F32) * ag_ref[...].astype(F32)).astype(BF16)
    y_a = jnp.dot(a, watt_ref[...], preferred_element_type=F32)
    s = _gelu_tanh(ys_ref[...].astype(F32) + d_ref[...] * u_ref[...])
    z = jnp.dot(s.astype(BF16), wglu_ref[...], preferred_element_type=F32) + bglu_ref[...]
    s = s * jax.nn.sigmoid(z)
    y_b = jnp.dot((s * sg_ref[...]).astype(BF16), wssm_ref[...], preferred_element_type=F32)
    gates = gate_ref[...].astype(F32)
    mixed = gates[:, :D_MODEL] * y_a + gates[:, D_MODEL:] * y_b
    r = x_ref[...] + jnp.dot(mixed.astype(BF16), wout_ref[...], preferred_element_type=F32)
    y = r * lax.rsqrt(jnp.mean(r * r, axis=-1, keepdims=True) + EPS)
    o_ref[...] = (y * gfin_ref[...]).astype(o_ref.dtype)


def _out_stage(att, att_gate, y_ssm, u_gate, gates, x, d_skip, w_att_out, w_glu, b_glu, w_ssm_out, w_out,
               g_final, tm=512):
    l = x.shape[0]
    row = lambda n: pl.BlockSpec((tm, n), lambda i: (i, 0))
    return pl.pallas_call(
        _out_kernel,
        grid=(l // tm,),
        in_specs=[row(ATT_WIDTH), row(ATT_WIDTH), row(SSM_WIDTH), row(SSM_WIDTH),
                  pl.BlockSpec((tm, SSM_WIDTH), lambda i: (i, 1)),
                  row(N_BRANCH * D_MODEL), row(D_MODEL), _resident((1, SSM_WIDTH)),
                  _resident((ATT_WIDTH, D_MODEL)), _resident((SSM_WIDTH, SSM_WIDTH)), _resident((1, SSM_WIDTH)),
                  _resident((SSM_WIDTH, D_MODEL)), _resident((D_MODEL, D_MODEL)), _resident((1, D_MODEL))],
        out_specs=row(D_MODEL),
        out_shape=jax.ShapeDtypeStruct((l, D_MODEL), x.dtype),
        compiler_params=_params(("parallel",)),
        name="out_stage",
    )(att, att_gate, y_ssm, u_gate, u_gate, gates, x, d_skip.astype(F32).reshape(1, -1), w_att_out.astype(BF16),
      w_glu.astype(BF16), b_glu.reshape(1, -1), w_ssm_out.astype(BF16), w_out.astype(BF16), g_final.reshape(1, -1))


def _layer(x, positions, g_norm, w_in, b_merge, a_re, a_im, log_dt, ssm_b_re, ssm_b_im, ssm_c_re, ssm_c_im,
           ssm_d, w_glu, b_glu, w_att_out, w_ssm_out, w_out, g_out):
    off = [0]
    for s in SPLITS:
        off.append(off[-1] + s)
    head_end = off[7] + LANES - IDX_DIM - IDX_HEADS
    w_qkv = w_in[:, :off[3]].astype(BF16)
    w_mid = w_in[:, off[3]:head_end].astype(BF16)
    w_tail = w_in[:, off[7]:].astype(BF16)
    rope_att = _rope_tables(positions, HEAD_DIM)
    rope_idx = _rope_tables(positions, IDX_DIM)

    q_scale = HEAD_DIM ** -0.5 * LOG2E
    qkv_t, h = _project("qkv", x, w_qkv, 0, off[3],
                        ((off[0], ATT_WIDTH, "rope128", q_scale), (off[1], KV_WIDTH, "rope128", 1.0),
                         (off[2], KV_WIDTH, "none", 1.0)), BF16, rope_att, transpose_out=True, norm_gain=g_norm)
    q_t, kv_t = qkv_t, qkv_t[ATT_WIDTH:]
    att_gate, q_idx_t, misc_t = _project(
        "gate_idx", h, w_mid, 0, head_end - off[3],
        ((0, ATT_WIDTH, "silu", 1.0, 0), (off[4] - off[3], IDX_HEADS * IDX_DIM, "rope64", 1.0, 1),
         (off[5] - off[3], LANES, "idx_misc", IDX_HEADS ** -0.5 * IDX_DIM ** -0.5, 2)),
        (BF16, BF16, F32), rope_idx, transpose_out=(False, True, True))
    u_gate = _project("ssm", h, w_tail, 0, 2 * SSM_WIDTH,
                      ((0, SSM_WIDTH, "none", 1.0), (SSM_WIDTH, SSM_WIDTH, "silu", 1.0)), F32)
    gates = _project("merge", h, w_tail, 2 * SSM_WIDTH, N_BRANCH * D_MODEL, ((0, 1024, "sigmoid", 1.0),), BF16,
                     bias=b_merge, tm=1024, tn=1024)

    k_idx = misc_t[:IDX_DIM].T.astype(BF16)
    w_idx_t = misc_t[IDX_DIM:IDX_DIM + IDX_HEADS]
    att = _dsa(q_t, kv_t[:KV_WIDTH], kv_t[KV_WIDTH:], q_idx_t, k_idx, w_idx_t, BF16)
    y_ssm = _s5(u_gate[:, :SSM_WIDTH], a_re, a_im, log_dt, ssm_b_re, ssm_b_im, ssm_c_re, ssm_c_im)
    return _out_stage(att, att_gate, y_ssm, u_gate, gates, x, ssm_d, w_att_out, w_glu, b_glu, w_ssm_out, w_out, g_out)


def kernel(x, positions, g_norm, w_in, b_merge, a_re, a_im, log_dt, ssm_b_re, ssm_b_im, ssm_c_re, ssm_c_im,
           ssm_d, w_glu, b_glu, w_att_out, w_ssm_out, w_out, g_final):
    depth = w_in.shape[0]
    assert depth == 1 and x.shape[0] == 1, "single layer, single sequence"
    y = _layer(x[0], positions[0], g_norm[0], w_in[0], b_merge[0], a_re[0], a_im[0], log_dt[0], ssm_b_re[0],
               ssm_b_im[0], ssm_c_re[0], ssm_c_im[0], ssm_d[0], w_glu[0], b_glu[0], w_att_out[0], w_ssm_out[0],
               w_out[0], g_final)
    return y[None]
```
